```python
import math
import jax
import jax.numpy as jnp
from jax import lax
import numpy as np

D_MODEL = 1024
BATCH = 2
SEQ = 16384
DEPTH = 2

RW_HEADS = 8
RW_HEAD_DIM = 64
RW_WIDTH = RW_HEADS * RW_HEAD_DIM
RW_DECAY_LORA = 64
RW_AAA_LORA = 64
RW_GATE_LORA = 128
RW_MV_LORA = 32
RW_GN_EPS = 64e-5
SSM_HEADS = 8
SSM_HEAD_DIM = 64
SSM_INNER = SSM_HEADS * SSM_HEAD_DIM
SSM_GROUPS = 2
SSM_STATE = 128
SSM_CONV = 4
SSM_CHUNK = 128
SSM_CONV_DIM = SSM_INNER + 2 * SSM_GROUPS * SSM_STATE
SSM_NORM_GROUP = SSM_INNER // SSM_GROUPS
RET_HEADS = 8
RET_HEAD_DIM = 64
RET_WIDTH = RET_HEADS * RET_HEAD_DIM
RET_CHUNK = 128
ROPE_BASE = 10000.0
N_BRANCH = 3
BRANCH_WIDTH = 512
D_FF = 2816
N_EXPERTS = 8
TOP_K = 2
D_FF_EXPERT = 3584
N_DENSE = (DEPTH + 1) // 2
N_MOE = DEPTH // 2
RMS_EPS = 1e-6

RW_COLS = 3 * RW_WIDTH + RW_DECAY_LORA + RW_AAA_LORA + RW_GATE_LORA
SSM_COLS = SSM_INNER + SSM_CONV_DIM + SSM_HEADS
RET_COLS = 4 * RET_WIDTH
GATE_COLS = N_BRANCH * D_MODEL
IN_COLS = RW_COLS + SSM_COLS + RET_COLS + GATE_COLS

kernel_name = 'hybrid_rwkv7_ssd_retention_moe'


def rms_norm(x, w):
    xf = x.astype(jnp.float32)
    y = xf * lax.rsqrt(jnp.mean(xf * xf, axis=-1, keepdims=True) + RMS_EPS)
    return (y * w.astype(jnp.float32)).astype(x.dtype)


def token_shift(x):
    return jnp.pad(x, ((0, 0), (1, 0), (0, 0)))[:, :-1]


def rwkv7_mix(p, mu, w_up, w0, a_up, a0, g_up, k_k, k_a, r_k, gn_w, gn_b, v_first, vres):
    bsz, T, _ = p.shape
    xm = p + (token_shift(p) - p) * mu
    o1, o2, o3 = RW_WIDTH, 2 * RW_WIDTH, 3 * RW_WIDTH
    o4 = o3 + RW_DECAY_LORA
    o5 = o4 + RW_AAA_LORA
    r, k, v = xm[..., :o1], xm[..., o1:o2], xm[..., o2:o3]
    xw, xa, xg = xm[..., o3:o4], xm[..., o4:o5], xm[..., o5:]
    log_decay = -jnp.exp(-jax.nn.softplus(-(w0 + jnp.tanh(xw) @ w_up)) - 0.5)
    a = jax.nn.sigmoid(a0 + xa @ a_up)
    g = jax.nn.sigmoid(xg) @ g_up
    if vres is None:
        v_first = v
    else:
        v_down, v_up, v0 = vres
        v = v + (v_first - v) * jax.nn.sigmoid(v0 + (v @ v_down) @ v_up)
    heads = lambda t: t.reshape(bsz, T, RW_HEADS, RW_HEAD_DIM)
    kk = heads(k * k_k)
    kk = kk * lax.rsqrt(jnp.maximum(jnp.sum(kk * kk, -1, keepdims=True), 1e-24))
    k = k * (1.0 + (a - 1.0) * k_a)
    r_h, k_h, v_h, a_h = heads(r), heads(k), heads(v), heads(a)
    seq_in = tuple(jnp.moveaxis(t, 1, 0) for t in
                   (r_h, heads(jnp.exp(log_decay)), k_h, v_h, -kk, kk * a_h))

    def step(state, inp):
        r_t, w_t, k_t, v_t, a_t, b_t = inp
        sa = jnp.einsum('bhij,bhj->bhi', state, a_t)
        state = (state * w_t[:, :, None, :] + sa[..., None] * b_t[:, :, None, :]
                 + v_t[..., None] * k_t[:, :, None, :])
        return state, jnp.einsum('bhij,bhj->bhi', state, r_t)

    s0 = jnp.zeros((bsz, RW_HEADS, RW_HEAD_DIM, RW_HEAD_DIM), jnp.float32)
    _, y = lax.scan(step, s0, seq_in, unroll=8)
    y = jnp.moveaxis(y, 0, 1)
    mean = jnp.mean(y, -1, keepdims=True)
    var = jnp.mean(jnp.square(y - mean), -1, keepdims=True)
    y = ((y - mean) * lax.rsqrt(var + RW_GN_EPS)).reshape(bsz, T, RW_WIDTH) * gn_w + gn_b
    bonus = jnp.sum(r_h * k_h * r_k, -1, keepdims=True) * v_h
    out = (y + bonus.reshape(bsz, T, RW_WIDTH)) * g
    return out, v_first


def causal_depthwise_conv(x, w, b):
    C = x.shape[-1]
    rhs = jnp.transpose(w).astype(x.dtype)[:, None, :]
    y = lax.conv_general_dilated(x, rhs, window_strides=(1,), padding=[(SSM_CONV - 1, 0)],
                                 dimension_numbers=('NWC', 'WIO', 'NWC'), feature_group_count=C)
    return y + b


def ssd_chunked(x, a, b, c):
    bsz, T, H, P = x.shape
    G, N = b.shape[-2], b.shape[-1]
    HG = H // G
    Q = SSM_CHUNK
    nc = T // Q
    x = x.reshape(bsz, nc, Q, G, HG, P)
    b = b.reshape(bsz, nc, Q, G, N)
    c = c.reshape(bsz, nc, Q, G, N)
    a = a.reshape(bsz, nc, Q, G, HG).transpose(0, 3, 4, 1, 2)
    a_cs = jnp.cumsum(a, axis=-1)
    causal = jnp.tril(jnp.ones((Q, Q), bool))
    seg = a_cs[..., :, None] - a_cs[..., None, :]
    L = jnp.where(causal, jnp.exp(jnp.where(causal, seg, 0.0)), 0.0)
    cb = jnp.einsum('bclgn,bcsgn->bgcls', c, b)
    y_diag = jnp.einsum('bghcls,bcsghp->bclghp', cb[:, :, None] * L, x)
    decay_states = jnp.exp(a_cs[..., -1:] - a_cs)
    states = jnp.einsum('bclgn,bghcl,bclghp->bcghpn', b, decay_states, x)
    chunk_decay = jnp.exp(a_cs[..., -1])

    def step(s, inp):
        st, dec = inp
        return dec[..., None, None] * s + st, s

    s0 = jnp.zeros((bsz, G, HG, P, N), x.dtype)
    _, prev = lax.scan(step, s0, (jnp.moveaxis(states, 1, 0), jnp.moveaxis(chunk_decay, -1, 0)))
    prev = jnp.moveaxis(prev, 0, 1)
    y_off = jnp.einsum('bclgn,bcghpn,bghcl->bclghp', c, prev, jnp.exp(a_cs))
    return (y_diag + y_off).reshape(bsz, T, H, P)


def mamba2_mix(p, conv_w, conv_b, dt_bias, a_log, d_skip, norm_w):
    bsz, T, _ = p.shape
    z = p[..., :SSM_INNER]
    xbc = p[..., SSM_INNER:SSM_INNER + SSM_CONV_DIM]
    dt = p[..., SSM_INNER + SSM_CONV_DIM:]
    xbc = jax.nn.silu(causal_depthwise_conv(xbc, conv_w, conv_b))
    nb = SSM_GROUPS * SSM_STATE
    xs = xbc[..., :SSM_INNER].reshape(bsz, T, SSM_HEADS, SSM_HEAD_DIM)
    bm = xbc[..., SSM_INNER:SSM_INNER + nb].reshape(bsz, T, SSM_GROUPS, SSM_STATE)
    cm = xbc[..., SSM_INNER + nb:].reshape(bsz, T, SSM_GROUPS, SSM_STATE)
    dt = jax.nn.softplus(dt + dt_bias)
    A = -jnp.exp(a_log)
    y = ssd_chunked(xs * dt[..., None], A * dt, bm, cm)
    y = y + d_skip[:, None] * xs
    y = y.reshape(bsz, T, SSM_INNER) * jax.nn.silu(z)
    yg = y.reshape(bsz, T, SSM_GROUPS, SSM_NORM_GROUP)
    yg = yg * lax.rsqrt(jnp.mean(yg * yg, -1, keepdims=True) + RMS_EPS)
    return yg.reshape(bsz, T, SSM_INNER) * norm_w


def rotary(t, positions):
    half = RET_HEAD_DIM // 2
    inv_freq = ROPE_BASE ** (-jnp.arange(half, dtype=jnp.float32) / half)
    ang = positions.astype(jnp.float32)[:, None] * inv_freq[None, :]
    cos = jnp.cos(ang)[None, :, None, :]
    sin = jnp.sin(ang)[None, :, None, :]
    t1, t2 = t[..., 0::2], t[..., 1::2]
    return jnp.stack([t1 * cos - t2 * sin, t1 * sin + t2 * cos], axis=-1).reshape(t.shape)


def retention_mix(p, positions):
    bsz, T, _ = p.shape
    H, Dh, Q = RET_HEADS, RET_HEAD_DIM, RET_CHUNK
    nc = T // Q
    q = rotary(p[..., :RET_WIDTH].reshape(bsz, T, H, Dh), positions)
    k = rotary(p[..., RET_WIDTH:2 * RET_WIDTH].reshape(bsz, T, H, Dh), positions) * (Dh ** -0.5)
    v = p[..., 2 * RET_WIDTH:3 * RET_WIDTH].reshape(bsz, T, H, Dh)
    g = p[..., 3 * RET_WIDTH:]
    log_gamma = jnp.log1p(-jnp.exp2(-5.0 - jnp.arange(H, dtype=jnp.float32)))
    q, k, v = (t.reshape(bsz, nc, Q, H, Dh) for t in (q, k, v))
    idx = jnp.arange(Q, dtype=jnp.float32)
    rel = idx[:, None] - idx[None, :]
    causal = rel >= 0
    decay_in = jnp.where(causal, jnp.exp(jnp.where(causal, rel, 0.0)[None] * log_gamma[:, None, None]), 0.0)
    scores = jnp.einsum('bclhd,bcshd->bchls', q, k) * decay_in
    y_in = jnp.einsum('bchls,bcshe->bclhe', scores, v)
    zeta = jnp.exp((Q - 1 - idx)[None, :] * log_gamma[:, None])
    kv = jnp.einsum('bcshd,hs,bcshe->bchde', k, zeta, v)
    chunk_decay = jnp.exp(Q * log_gamma)[:, None, None]

    def step(state, kv_c):
        return chunk_decay * state + kv_c, state

    _, prev = lax.scan(step, jnp.zeros((bsz, H, Dh, Dh), jnp.float32), jnp.moveaxis(kv, 1, 0))
    prev = jnp.moveaxis(prev, 0, 1)
    xi = jnp.exp((idx + 1.0)[None, :] * log_gamma[:, None])
    y_cross = jnp.einsum('bclhd,bchde,hl->bclhe', q, prev, xi)
    y = (y_in + y_cross).reshape(bsz, T, H, Dh)
    y = y * lax.rsqrt(jnp.mean(y * y, -1, keepdims=True) + RMS_EPS)
    return y.reshape(bsz, T, RET_WIDTH) * jax.nn.silu(g)


def swiglu(x, wg, wu, wd):
    return (jax.nn.silu(x @ wg) * (x @ wu)) @ wd


def moe_swiglu(x, router, wg, wu, wd):
    logits = (x @ router).astype(jnp.float32)
    top_v, top_i = lax.top_k(logits, TOP_K)
    top_w = jax.nn.softmax(top_v, axis=-1)
    gates = jnp.sum(jax.nn.one_hot(top_i, N_EXPERTS, dtype=jnp.float32) * top_w[..., None], axis=-2)
    gates = gates.astype(x.dtype)
    out = jnp.zeros_like(x)
    for e in range(N_EXPERTS):
        out = out + gates[..., e:e + 1] * swiglu(x, wg[e], wu[e], wd[e])
    return out


def setup_inputs(seed: int = 0) -> dict:
    key = jax.random.key(seed)
    ks = iter(jax.random.split(key, 48))
    f32 = jnp.float32

    def nrm(shape, scale):
        return jax.random.normal(next(ks), shape, f32) * scale

    def unif(shape, lo, hi):
        return jax.random.uniform(next(ks), shape, f32, lo, hi)

    L, Lv = DEPTH, DEPTH - 1
    dt_init = jnp.exp(unif((L, SSM_HEADS), math.log(1e-3), math.log(1e-1)))
    return {
        'x': nrm((BATCH, SEQ, D_MODEL), 1.0),
        'norm1_w': 1.0 + nrm((L, D_MODEL), 0.1),
        'w_in': nrm((L, D_MODEL, IN_COLS), D_MODEL ** -0.5),
        'rw_mu': unif((L, RW_COLS), 0.0, 1.0),
        'rw_w_up': nrm((L, RW_DECAY_LORA, RW_WIDTH), 0.1 * RW_DECAY_LORA ** -0.5),
        'rw_w0': unif((L, RW_WIDTH), -4.0, 1.0),
        'rw_a_up': nrm((L, RW_AAA_LORA, RW_WIDTH), 0.5 * RW_AAA_LORA ** -0.5),
        'rw_a0': nrm((L, RW_WIDTH), 0.5),
        'rw_g_up': nrm((L, RW_GATE_LORA, RW_WIDTH), RW_GATE_LORA ** -0.5),
        'rw_k_k': 0.85 + nrm((L, RW_WIDTH), 0.1),
        'rw_k_a': 1.0 + nrm((L, RW_WIDTH), 0.1),
        'rw_r_k': nrm((L, RW_HEADS, RW_HEAD_DIM), 0.1),
        'rw_gn_w': 1.0 + nrm((L, RW_WIDTH), 0.1),
        'rw_gn_b': nrm((L, RW_WIDTH), 0.01),
        'rw_v_down': nrm((Lv, RW_WIDTH, RW_MV_LORA), RW_WIDTH ** -0.5),
        'rw_v_up': nrm((Lv, RW_MV_LORA, RW_WIDTH), 0.5 * RW_MV_LORA ** -0.5),
        'rw_v0': 1.0 + nrm((Lv, RW_WIDTH), 0.1),
        'ssm_conv_w': nrm((L, SSM_CONV_DIM, SSM_CONV), SSM_CONV ** -0.5),
        'ssm_conv_b': nrm((L, SSM_CONV_DIM), 0.01),
        'ssm_dt_bias': dt_init + jnp.log(-jnp.expm1(-dt_init)),
        'ssm_a_log': jnp.log(unif((L, SSM_HEADS), 1.0, 16.0)),
        'ssm_d': 1.0 + nrm((L, SSM_HEADS), 0.1),
        'ssm_norm_w': 1.0 + nrm((L, SSM_INNER), 0.1),
        'w_branch': nrm((L, N_BRANCH, BRANCH_WIDTH, D_MODEL), BRANCH_WIDTH ** -0.5),
        'w_out': nrm((L, D_MODEL, D_MODEL), D_MODEL ** -0.5),
        'norm2_w': 1.0 + nrm((L, D_MODEL), 0.1),
        'ffn_wg': nrm((N_DENSE, D_MODEL, D_FF), D_MODEL ** -0.5),
        'ffn_wu': nrm((N_DENSE, D_MODEL, D_FF), D_MODEL ** -0.5),
        'ffn_wd': nrm((N_DENSE, D_FF, D_MODEL), D_FF ** -0.5),
        'moe_router': nrm((N_MOE, D_MODEL, N_EXPERTS), D_MODEL ** -0.5),
        'moe_wg': nrm((N_MOE, N_EXPERTS, D_MODEL, D_FF_EXPERT), D_MODEL ** -0.5),
        'moe_wu': nrm((N_MOE, N_EXPERTS, D_MODEL, D_FF_EXPERT), D_MODEL ** -0.5),
        'moe_wd': nrm((N_MOE, N_EXPERTS, D_FF_EXPERT, D_MODEL), D_FF_EXPERT ** -0.5),
        'final_norm_w': 1.0 + nrm((D_MODEL,), 0.1),
    }


def reference(x, norm1_w, w_in, rw_mu, rw_w_up, rw_w0, rw_a_up, rw_a0, rw_g_up, rw_k_k, rw_k_a,
              rw_r_k, rw_gn_w, rw_gn_b, rw_v_down, rw_v_up, rw_v0, ssm_conv_w, ssm_conv_b,
              ssm_dt_bias, ssm_a_log, ssm_d, ssm_norm_w, w_branch, w_out, norm2_w, ffn_wg,
              ffn_wu, ffn_wd, moe_router, moe_wg, moe_wu, moe_wd, final_norm_w):
    act_dtype = x.dtype
    f32 = jnp.float32
    positions = jnp.arange(x.shape[1])
    c1 = RW_COLS
    c2 = c1 + SSM_COLS
    c3 = c2 + RET_COLS
    v_first = None
    for layer in range(DEPTH):
        h = rms_norm(x, norm1_w[layer])
        wl = w_in[layer]
        p_rw = (h @ wl[:, :c1]).astype(f32)
        p_ssm = (h @ wl[:, c1:c2]).astype(f32)
        p_ret = (h @ wl[:, c2:c3]).astype(f32)
        gate_logits = h @ wl[:, c3:]
        vres = None if layer == 0 else (rw_v_down[layer - 1], rw_v_up[layer - 1], rw_v0[layer - 1])
        y_rw, v_first = rwkv7_mix(p_rw, rw_mu[layer], rw_w_up[layer], rw_w0[layer], rw_a_up[layer],
                                  rw_a0[layer], rw_g_up[layer], rw_k_k[layer], rw_k_a[layer],
                                  rw_r_k[layer], rw_gn_w[layer], rw_gn_b[layer], v_first, vres)
        y_ssm = mamba2_mix(p_ssm, ssm_conv_w[layer], ssm_conv_b[layer], ssm_dt_bias[layer],
                           ssm_a_log[layer], ssm_d[layer], ssm_norm_w[layer])
        y_ret = retention_mix(p_ret, positions)
        merged = None
        for i, y_b in enumerate((y_rw, y_ssm, y_ret)):
            gate = jax.nn.sigmoid(gate_logits[..., i * D_MODEL:(i + 1) * D_MODEL].astype(f32)).astype(act_dtype)
            term = gate * (y_b.astype(act_dtype) @ w_branch[layer, i])
            merged = term if merged is None else merged + term
        x = x + merged @ w_out[layer]
        h2 = rms_norm(x, norm2_w[layer])
        j = layer // 2
        if layer % 2 == 0:
            x = x + swiglu(h2, ffn_wg[j], ffn_wu[j], ffn_wd[j])
        else:
            x = x + moe_swiglu(h2, moe_router[j], moe_wg[j], moe_wu[j], moe_wd[j])
    return rms_norm(x, final_norm_w)
```

```python
import functools
import math

import jax
import jax.numpy as jnp
from jax import lax
from jax.experimental import pallas as pl
from jax.experimental.pallas import tpu as pltpu

F32 = jnp.float32
BF16 = jnp.bfloat16

LANES = 128
SUBLANES = 8
VMEM_LIMIT_BYTES = 56 * 1024 * 1024

RMS_EPS = 1e-6
RW_GN_EPS = 64e-5
HEADS = 8
HEAD_DIM = 64
WIDTH = HEADS * HEAD_DIM
RW_LORA_PAD = 128
RW_MV_LORA = 32
SSM_GROUPS = 2
SSM_STATE = 128
SSM_CONV = 4
ROPE_BASE = 10000.0
N_EXPERTS = 8

RW_CHUNK = 64
SSM_CHUNK = 128
RET_CHUNK = 128


def _cparams(*sem):
    return pltpu.CompilerParams(dimension_semantics=sem, vmem_limit_bytes=VMEM_LIMIT_BYTES)


def _dot(a, b):
    return jnp.dot(a.astype(BF16), b.astype(BF16), preferred_element_type=F32)


def _dot_nt(a, b):
    return lax.dot_general(a.astype(BF16), b.astype(BF16), (((1,), (1,)), ((), ())),
                           preferred_element_type=F32)


def _dot_tn(a, b):
    return lax.dot_general(a.astype(BF16), b.astype(BF16), (((0,), (0,)), ((), ())),
                           preferred_element_type=F32)


def _split2(x):
    hi = x.astype(BF16)
    lo = (x - hi.astype(F32)).astype(BF16)
    return hi, lo


def _split3(x):
    hi = x.astype(BF16)
    r = x - hi.astype(F32)
    mid = r.astype(BF16)
    lo = (r - mid.astype(F32)).astype(BF16)
    return hi, mid, lo


def _dot_exact_rhs(x, m_bf16):
    hi, mid, lo = _split3(x)
    d = lambda a: jnp.dot(a, m_bf16, preferred_element_type=F32)
    return d(hi) + d(mid) + d(lo)


def _dot_exact_lhs(m_bf16, x):
    hi, mid, lo = _split3(x)
    d = lambda a: jnp.dot(m_bf16, a, preferred_element_type=F32)
    return d(hi) + d(mid) + d(lo)


def _sigmoid(x):
    return 1.0 / (1.0 + jnp.exp(-x))


def _silu(x):
    return x * _sigmoid(x)


def _softplus(x):
    return jnp.maximum(x, 0.0) + jnp.log(1.0 + jnp.exp(-jnp.abs(x)))


def _rms_rows(x, w):
    return x * lax.rsqrt(jnp.mean(x * x, axis=-1, keepdims=True) + RMS_EPS) * w


def _norm_proj_kernel(x_ref, nw_ref, w_ref, o_ref):
    h = _rms_rows(x_ref[...], nw_ref[...]).astype(BF16)
    o_ref[...] = jnp.dot(h, w_ref[...], preferred_element_type=F32)


def _norm_proj(x2, norm_w, w_bf16, tm):
    n, d = x2.shape
    c = w_bf16.shape[1]
    return pl.pallas_call(
        _norm_proj_kernel,
        grid=(n // tm,),
        in_specs=[pl.BlockSpec((tm, d), lambda i: (i, 0)),
                  pl.BlockSpec((1, d), lambda i: (0, 0)),
                  pl.BlockSpec((d, c), lambda i: (0, 0))],
        out_specs=pl.BlockSpec((tm, c), lambda i: (i, 0)),
        out_shape=jax.ShapeDtypeStruct((n, c), F32),
        compiler_params=_cparams("parallel"),
        name="norm_proj",
    )(x2, norm_w.reshape(1, d), w_bf16)


def _head_sum(x, blk_ref):
    return _dot_exact_rhs(x, blk_ref[...])


def _rwkv_a_kernel(*refs, tt, has_vres):
    if has_vres:
        (p_ref, pprev_ref, mu_ref, wup_ref, w0_ref, aup_ref, a0_ref, gup_ref, kk_ref, ka_ref,
         rk_ref, blk_ref, vfirst_ref, vdown_ref, vup_ref, v0_ref,
         ry_ref, mc_ref, bonus_ref, g_ref,
         at_s, rt_s, bt_s, kt_s, bh_s, kh_s, v_s, wl_s) = refs
        vout_ref = None
    else:
        (p_ref, pprev_ref, mu_ref, wup_ref, w0_ref, aup_ref, a0_ref, gup_ref, kk_ref, ka_ref,
         rk_ref, blk_ref,
         ry_ref, mc_ref, bonus_ref, g_ref, vout_ref,
         at_s, rt_s, bt_s, kt_s, bh_s, kh_s, v_s, wl_s) = refs
    L = RW_CHUNK
    nchunk = tt // L
    W = WIDTH

    p = p_ref[0]
    prev = jnp.where(pl.program_id(1) == 0, 0.0, pprev_ref[0][SUBLANES - 1:SUBLANES, :])
    row = lax.broadcasted_iota(jnp.int32, (tt, 1), 0)
    shifted = jnp.where(row == 0, prev, pltpu.roll(p, 1, axis=0))
    xm = p + (shifted - p) * mu_ref[...]

    r = xm[:, 0:W]
    k = xm[:, W:2 * W]
    v = xm[:, 2 * W:3 * W]
    xwa = xm[:, 3 * W:3 * W + RW_LORA_PAD]
    xg = xm[:, 3 * W + RW_LORA_PAD:]

    ld = -math.exp(-0.5) * _sigmoid(w0_ref[...] + _dot(jnp.tanh(xwa), wup_ref[...]))
    a_sig = _sigmoid(a0_ref[...] + _dot(xwa, aup_ref[...]))
    g_ref[0] = _dot(_sigmoid(xg), gup_ref[...])
    if has_vres:
        vu = _dot(_dot(v, vdown_ref[...]), vup_ref[...])
        v = v + (vfirst_ref[0] - v) * _sigmoid(v0_ref[...] + vu)
    else:
        vout_ref[0] = v

    kk = k * kk_ref[...]
    kk = kk * lax.rsqrt(jnp.maximum(_head_sum(kk * kk, blk_ref), 1e-24))
    k = k * (1.0 + (a_sig - 1.0) * ka_ref[...])
    bonus_ref[0] = _head_sum(r * k * rk_ref[...], blk_ref) * v
    avec = -kk
    bvec = kk * a_sig

    cum = ld
    pos = row & (L - 1)
    d = 1
    while d < L:
        cum = cum + jnp.where(pos >= d, pltpu.roll(cum, d, axis=0), 0.0)
        d *= 2
    cum3 = cum.reshape(nchunk, L, W)
    last = jnp.broadcast_to(cum3[:, L - 1:L, :], (nchunk, L, W)).reshape(tt, W)
    w_inv = jnp.exp(-cum)
    w_last = jnp.exp(last - cum)

    def put(dst, val):
        for hp in range(HEADS // 2):
            dst[hp] = val[:, hp * LANES:(hp + 1) * LANES].astype(dst.dtype)

    put(at_s, avec * jnp.exp(cum - ld))
    put(rt_s, r * jnp.exp(cum))
    put(bt_s, bvec * w_inv)
    put(kt_s, k * w_inv)
    put(bh_s, bvec * w_last)
    put(kh_s, k * w_last)
    put(v_s, v)
    put(wl_s, jnp.exp(last))

    ri = lax.broadcasted_iota(jnp.int32, (L, L), 0)
    ci = lax.broadcasted_iota(jnp.int32, (L, L), 1)
    strict = ri > ci
    incl = ri >= ci
    eye = ri == ci

    def chunk_body(c, carry):
        r0 = pl.multiple_of(c * L, L)
        for hp in range(HEADS // 2):
            ld2 = lambda s: s[hp, pl.ds(r0, L), :]
            at2, rt2, bt2, kt2, bh2, kh2, v2 = (ld2(s) for s in (at_s, rt_s, bt_s, kt_s, bh_s, kh_s, v_s))
            wl2 = wl_s[hp, pl.ds(r0 + L - 1, 1), :]
            for sub in range(2):
                h = 2 * hp + sub
                sl = slice(sub * HEAD_DIM, (sub + 1) * HEAD_DIM)
                at, rt, bt, kt, bh, kh, vv = (z[:, sl] for z in (at2, rt2, bt2, kt2, bh2, kh2, v2))
                g4 = _dot_nt(jnp.concatenate([at, rt], axis=0), jnp.concatenate([bt, kt], axis=0))
                nmat = jnp.where(strict, g4[0:L, 0:L], 0.0)
                a_ak = jnp.where(strict, g4[0:L, L:2 * L], 0.0)
                a_rb = jnp.where(incl, g4[L:2 * L, 0:L], 0.0)
                a_rk = jnp.where(incl, g4[L:2 * L, L:2 * L], 0.0)
                tp = nmat
                npow = nmat
                span = 1
                while span * 2 < L:
                    npow = _dot(npow, npow)
                    tp = tp + npow + _dot(tp, npow)
                    span *= 2
                av = _dot(jnp.concatenate([a_ak, a_rk], axis=0), vv)
                x0 = jnp.concatenate([at.astype(F32), av[0:L]], axis=1)
                x = x0 + _dot(tp, x0)
                ry = jnp.concatenate([rt.astype(F32), av[L:2 * L]], axis=1) + _dot(a_rb, x)
                bx = _dot_tn(bh, x)
                kv = _dot_tn(kh, vv)
                diag = jnp.where(eye, wl2[:, sl], 0.0)
                mc = bx + jnp.concatenate([diag, kv], axis=1)
                ry_ref[0, h, pl.ds(r0, L), :] = ry
                mc_ref[0, h, pl.ds(r0, L), :] = mc
        return carry

    lax.fori_loop(0, nchunk, chunk_body, 0)


def _rwkv_b_kernel(ry_ref, mc_ref, bonus_ref, g_ref, gnw_ref, gnb_ref, blk_ref, o_ref, st_s, y_s, *, tt):
    L = RW_CHUNK
    nchunk = tt // L

    @pl.when(pl.program_id(1) == 0)
    def _():
        st_s[...] = jnp.zeros_like(st_s)

    def chunk_body(c, carry):
        r0 = pl.multiple_of(c * L, L)
        for h in range(HEADS):
            st = st_s[h]
            ry = ry_ref[0, h, pl.ds(r0, L), :]
            mc = mc_ref[0, h, pl.ds(r0, L), :]
            y = ry[:, HEAD_DIM:] + _dot(ry[:, :HEAD_DIM], st)
            m_hi, m_lo = _split2(mc[:, :HEAD_DIM])
            s_hi, s_lo = _split2(st)
            d = lambda a, b: jnp.dot(a, b, preferred_element_type=F32)
            st_s[h] = mc[:, HEAD_DIM:] + d(m_hi, s_hi) + d(m_lo, s_hi) + d(m_hi, s_lo)
            y_s[pl.ds(r0, L), h * HEAD_DIM:(h + 1) * HEAD_DIM] = y
        return carry

    lax.fori_loop(0, nchunk, chunk_body, 0)

    y = y_s[...]
    inv = 1.0 / HEAD_DIM
    mean = _head_sum(y, blk_ref) * inv
    dlt = y - mean
    var = _head_sum(dlt * dlt, blk_ref) * inv
    yn = dlt * lax.rsqrt(var + RW_GN_EPS) * gnw_ref[...] + gnb_ref[...]
    o_ref[0] = ((yn + bonus_ref[0]) * g_ref[0]).astype(o_ref.dtype)


def _rwkv_mix(p_rw, prm, v_first, vres, tt):
    bsz, t, cols = p_rw.shape
    W = WIDTH
    has_vres = vres is not None
    row = lambda a: a.reshape(1, -1).astype(F32)
    full = lambda shape: pl.BlockSpec(shape, lambda b, i: (0,) * len(shape))
    tile = lambda c: pl.BlockSpec((1, tt, c), lambda b, i: (b, i, 0))
    blk = (jnp.arange(W)[:, None] // HEAD_DIM == jnp.arange(W)[None, :] // HEAD_DIM).astype(BF16)

    zpad = jnp.zeros((RW_LORA_PAD // 2, W), F32)
    wup = jnp.concatenate([prm['w_up'], zpad], axis=0).astype(BF16)
    aup = jnp.concatenate([zpad, prm['a_up']], axis=0).astype(BF16)
    ins = [p_rw, p_rw, row(prm['mu']), wup, row(prm['w0']), aup, row(prm['a0']),
           prm['g_up'].astype(BF16), row(prm['k_k']), row(prm['k_a']), row(prm['r_k']), blk]
    in_specs = [tile(cols),
                pl.BlockSpec((1, SUBLANES, cols),
                             lambda b, i: (b, jnp.maximum(i * (tt // SUBLANES) - 1, 0), 0)),
                full((1, cols)), full((RW_LORA_PAD, W)), full((1, W)), full((RW_LORA_PAD, W)),
                full((1, W)), full((RW_LORA_PAD, W)), full((1, W)), full((1, W)), full((1, W)),
                full((W, W))]
    if has_vres:
        v_down, v_up, v0 = vres
        pad = LANES - RW_MV_LORA
        ins += [v_first, jnp.pad(v_down, ((0, 0), (0, pad))).astype(BF16),
                jnp.pad(v_up, ((0, pad), (0, 0))).astype(BF16), row(v0)]
        in_specs += [tile(W), full((W, LANES)), full((LANES, W)), full((1, W))]

    hm = jax.ShapeDtypeStruct((bsz, HEADS, t, 2 * HEAD_DIM), F32)
    tw = jax.ShapeDtypeStruct((bsz, t, W), F32)
    hm_spec = pl.BlockSpec((1, HEADS, tt, 2 * HEAD_DIM), lambda b, i: (b, 0, i, 0))
    out_shape = [hm, hm, tw, tw]
    out_specs = [hm_spec, hm_spec, tile(W), tile(W)]
    if not has_vres:
        out_shape.append(tw)
        out_specs.append(tile(W))
    pair = lambda dt: pltpu.VMEM((HEADS // 2, tt, LANES), dt)
    outs = pl.pallas_call(
        functools.partial(_rwkv_a_kernel, tt=tt, has_vres=has_vres),
        grid=(bsz, t // tt),
        in_specs=in_specs, out_specs=out_specs, out_shape=out_shape,
        scratch_shapes=[pair(BF16)] * 7 + [pair(F32)],
        compiler_params=_cparams("parallel", "parallel"),
        name="rwkv_a",
    )(*ins)
    if has_vres:
        ry, mc, bonus, g = outs
    else:
        ry, mc, bonus, g, v_first = outs

    y = pl.pallas_call(
        functools.partial(_rwkv_b_kernel, tt=tt),
        grid=(bsz, t // tt),
        in_specs=[hm_spec, hm_spec, tile(W), tile(W), full((1, W)), full((1, W)), full((W, W))],
        out_specs=tile(W),
        out_shape=jax.ShapeDtypeStruct((bsz, t, W), BF16),
        scratch_shapes=[pltpu.VMEM((HEADS, HEAD_DIM, HEAD_DIM), F32), pltpu.VMEM((tt, W), F32)],
        compiler_params=_cparams("parallel", "arbitrary"),
        name="rwkv_b",
    )(ry, mc, bonus, g, row(prm['gn_w']), row(prm['gn_b']), blk)
    return y, v_first


def _ssm_kernel(p_ref, pprev_ref, cw_ref, cb_ref, dtb_ref, alog_ref, dsk_ref, nw_ref, exp_ref,
                tri_ref, o_ref, st_s):
    Q = SSM_CHUNK
    W = WIDTH
    nb = SSM_GROUPS * SSM_STATE
    gw = W // SSM_GROUPS

    @pl.when(pl.program_id(1) == 0)
    def _():
        st_s[...] = jnp.zeros_like(st_s)

    p = p_ref[0]
    z = p[:, 0:W]
    xbc_in = p[:, W:2 * W + 2 * nb]
    dt_in = p[:, 2 * W + 2 * nb:]
    prev = jnp.where(pl.program_id(1) == 0, 0.0, pprev_ref[0][:, W:2 * W + 2 * nb])
    row = lax.broadcasted_iota(jnp.int32, (Q, 1), 0)
    acc = cb_ref[...] + cw_ref[SSM_CONV - 1:SSM_CONV, :] * xbc_in
    for j in range(1, SSM_CONV):
        sh = pltpu.roll(xbc_in, j, axis=0)
        for i in range(j):
            sh = jnp.where(row == i, prev[SUBLANES - j + i:SUBLANES - j + i + 1, :], sh)
        acc = acc + cw_ref[SSM_CONV - 1 - j:SSM_CONV - j, :] * sh
    xbc = _silu(acc)
    xs = xbc[:, 0:W]
    bm = xbc[:, W:W + nb]
    cm = xbc[:, W + nb:]

    dt = _softplus(dt_in + dtb_ref[...])
    a = -jnp.exp(alog_ref[...]) * dt
    a_cs = _dot_exact_lhs(tri_ref[...], a)
    a_cs_t = a_cs.T
    dt_e = _dot_exact_rhs(dt, exp_ref[...])
    acs_e = _dot_exact_rhs(a_cs, exp_ref[...])
    last_e = acs_e[Q - 1:Q, :]
    xdt = xs * dt_e
    xdec = xdt * jnp.exp(last_e - acs_e)

    li = lax.broadcasted_iota(jnp.int32, (Q, Q), 0)
    si = lax.broadcasted_iota(jnp.int32, (Q, Q), 1)
    causal = li >= si
    hg = HEADS // SSM_GROUPS
    ys = []
    for g in range(SSM_GROUPS):
        bg = bm[:, g * SSM_STATE:(g + 1) * SSM_STATE]
        cg = cm[:, g * SSM_STATE:(g + 1) * SSM_STATE]
        cb = _dot_nt(cg, bg)
        for hh in range(hg):
            h = g * hg + hh
            seg = a_cs[:, h:h + 1] - a_cs_t[h:h + 1, :]
            lmat = jnp.where(causal, jnp.exp(jnp.where(causal, seg, 0.0)), 0.0)
            ys.append(_dot(cb * lmat, xdt[:, h * HEAD_DIM:(h + 1) * HEAD_DIM]))
        st = st_s[:, g * gw:(g + 1) * gw]
        y_off = _dot(cg, st) * jnp.exp(acs_e[:, g * gw:(g + 1) * gw])
        ys.append(y_off)
        st_s[:, g * gw:(g + 1) * gw] = (jnp.exp(last_e[:, g * gw:(g + 1) * gw]) * st
                                        + _dot_tn(bg, xdec[:, g * gw:(g + 1) * gw]))
    n5 = hg + 1
    y = jnp.concatenate(
        [jnp.concatenate(ys[g * n5:g * n5 + hg], axis=1) + ys[g * n5 + hg] for g in range(SSM_GROUPS)],
        axis=1)
    y = (y + dsk_ref[...] * xs) * _silu(z)
    outs = []
    for g in range(SSM_GROUPS):
        yg = y[:, g * gw:(g + 1) * gw]
        outs.append(yg * lax.rsqrt(jnp.mean(yg * yg, axis=-1, keepdims=True) + RMS_EPS))
    o_ref[0] = (jnp.concatenate(outs, axis=1) * nw_ref[...]).astype(o_ref.dtype)


def _ssm_mix(p_ssm, prm):
    bsz, t, cols = p_ssm.shape
    W = WIDTH
    Q = SSM_CHUNK
    cdim = W + 2 * SSM_GROUPS * SSM_STATE
    row = lambda a: a.reshape(1, -1).astype(F32)
    padrow = lambda a: jnp.pad(a.astype(F32), (0, LANES - a.shape[0])).reshape(1, LANES)
    full = lambda shape: pl.BlockSpec(shape, lambda b, i: (0,) * len(shape))
    expand = (jnp.arange(LANES)[:, None] == jnp.arange(W)[None, :] // HEAD_DIM).astype(BF16)
    tri = (jnp.arange(Q)[:, None] >= jnp.arange(Q)[None, :]).astype(BF16)
    return pl.pallas_call(
        _ssm_kernel,
        grid=(bsz, t // Q),
        in_specs=[pl.BlockSpec((1, Q, cols), lambda b, i: (b, i, 0)),
                  pl.BlockSpec((1, SUBLANES, cols),
                               lambda b, i: (b, jnp.maximum(i * (Q // SUBLANES) - 1, 0), 0)),
                  full((SSM_CONV, cdim)), full((1, cdim)), full((1, LANES)), full((1, LANES)),
                  full((1, W)), full((1, W)), full((LANES, W)), full((Q, Q))],
        out_specs=pl.BlockSpec((1, Q, W), lambda b, i: (b, i, 0)),
        out_shape=jax.ShapeDtypeStruct((bsz, t, W), BF16),
        scratch_shapes=[pltpu.VMEM((SSM_STATE, W), F32)],
        compiler_params=_cparams("parallel", "arbitrary"),
        name="ssm",
    )(p_ssm, p_ssm, prm['conv_w'].T.astype(F32), row(prm['conv_b']), padrow(prm['dt_bias']),
      padrow(prm['a_log']), row(jnp.repeat(prm['d'], HEAD_DIM)), row(prm['norm_w']), expand, tri)


def _ret_log_gamma(h):
    return math.log1p(-(2.0 ** (-5.0 - h)))


def _ret_kernel(p_ref, freq_ref, sgn_ref, blk_ref, o_ref, st_s, dec_s, y_s):
    Q = RET_CHUNK
    W = WIDTH
    half = HEAD_DIM // 2

    @pl.when(pl.program_id(1) == 0)
    def _():
        st_s[...] = jnp.zeros_like(st_s)
        li = lax.broadcasted_iota(jnp.int32, (Q, Q), 0)
        si = lax.broadcasted_iota(jnp.int32, (Q, Q), 1)
        rel = (li - si).astype(F32)
        for h in range(HEADS):
            dec_s[h] = jnp.where(li >= si, jnp.exp(jnp.where(li >= si, rel, 0.0) * _ret_log_gamma(h)), 0.0)

    p = p_ref[0]
    idx = lax.broadcasted_iota(jnp.int32, (Q, 1), 0)
    pos = (pl.program_id(1) * Q + idx).astype(F32)
    ang = pos * freq_ref[...]
    cos = jnp.concatenate([jnp.cos(ang)] * (W // LANES), axis=1)
    sin = jnp.concatenate([jnp.sin(ang) * sgn_ref[...]] * (W // LANES), axis=1)
    lane = lax.broadcasted_iota(jnp.int32, (1, W), 1)
    first_half = (lane & (HEAD_DIM - 1)) < half

    def rot(x):
        partner = jnp.where(first_half, pltpu.roll(x, W - half, axis=1), pltpu.roll(x, half, axis=1))
        return x * cos + partner * sin

    q = rot(p[:, 0:W])
    k = rot(p[:, W:2 * W]) * (HEAD_DIM ** -0.5)
    v = p[:, 2 * W:3 * W]
    g = p[:, 3 * W:]
    idf = idx.astype(F32)
    for h in range(HEADS):
        lg = _ret_log_gamma(h)
        sl = slice(h * HEAD_DIM, (h + 1) * HEAD_DIM)
        qh, kh, vh = q[:, sl], k[:, sl], v[:, sl]
        scores = _dot_nt(qh, kh) * dec_s[h]
        st = st_s[h]
        y = _dot(scores, vh) + _dot(qh, st) * jnp.exp((idf + 1.0) * lg)
        st_s[h] = math.exp(Q * lg) * st + _dot_tn(kh * jnp.exp((Q - 1.0 - idf) * lg), vh)
        y_s[:, sl] = y
    y = y_s[...]
    ms = _head_sum(y * y, blk_ref) * (1.0 / HEAD_DIM)
    o_ref[0] = (y * lax.rsqrt(ms + RMS_EPS) * _silu(g)).astype(o_ref.dtype)


def _ret_mix(p_ret):
    bsz, t, cols = p_ret.shape
    W = WIDTH
    Q = RET_CHUNK
    half = HEAD_DIM // 2
    full = lambda shape: pl.BlockSpec(shape, lambda b, i: (0,) * len(shape))
    inv_freq = ROPE_BASE ** (-jnp.arange(half, dtype=F32) / half)
    freq = jnp.tile(inv_freq, LANES // half).reshape(1, LANES)
    sgn = jnp.where((jnp.arange(LANES) % HEAD_DIM) < half, -1.0, 1.0).astype(F32).reshape(1, LANES)
    blk = (jnp.arange(W)[:, None] // HEAD_DIM == jnp.arange(W)[None, :] // HEAD_DIM).astype(BF16)
    return pl.pallas_call(
        _ret_kernel,
        grid=(bsz, t // Q),
        in_specs=[pl.BlockSpec((1, Q, cols), lambda b, i: (b, i, 0)),
                  full((1, LANES)), full((1, LANES)), full((W, W))],
        out_specs=pl.BlockSpec((1, Q, W), lambda b, i: (b, i, 0)),
        out_shape=jax.ShapeDtypeStruct((bsz, t, W), BF16),
        scratch_shapes=[pltpu.VMEM((HEADS, HEAD_DIM, HEAD_DIM), F32), pltpu.VMEM((HEADS, Q, Q), F32),
                        pltpu.VMEM((Q, W), F32)],
        compiler_params=_cparams("parallel", "arbitrary"),
        name="retention",
    )(p_ret, freq, sgn, blk)


def _merge_kernel(x_ref, gl_ref, y0_ref, y1_ref, y2_ref, wb_ref, wo_ref, o_ref):
    d = x_ref.shape[1]
    merged = None
    for i, y_ref in enumerate((y0_ref, y1_ref, y2_ref)):
        gate = _sigmoid(gl_ref[:, i * d:(i + 1) * d])
        term = gate * jnp.dot(y_ref[...], wb_ref[i], preferred_element_type=F32)
        merged = term if merged is None else merged + term
    o_ref[...] = x_ref[...] + _dot(merged, wo_ref[...])


def _merge(x2, gate_logits, ys, w_branch, w_out, tm):
    n, d = x2.shape
    W = WIDTH
    rows = lambda c: pl.BlockSpec((tm, c), lambda i: (i, 0))
    return pl.pallas_call(
        _merge_kernel,
        grid=(n // tm,),
        in_specs=[rows(d), rows(3 * d), rows(W), rows(W), rows(W),
                  pl.BlockSpec((3, W, d), lambda i: (0, 0, 0)),
                  pl.BlockSpec((d, d), lambda i: (0, 0))],
        out_specs=rows(d),
        out_shape=jax.ShapeDtypeStruct((n, d), F32),
        compiler_params=_cparams("parallel"),
        name="merge_out",
    )(x2, gate_logits, *ys, w_branch.astype(BF16), w_out.astype(BF16))


def _ffn_kernel(*refs, n_experts, final_norm):
    if n_experts > 1:
        x_ref, nw_ref, rt_ref, wg_ref, wu_ref, wd_ref, fw_ref, o_ref, h_s, acc_s, gate_s = refs
    else:
        x_ref, nw_ref, wg_ref, wu_ref, wd_ref, fw_ref, o_ref, h_s, acc_s = refs
    e = pl.program_id(1)
    j = pl.program_id(2)

    @pl.when((e == 0) & (j == 0))
    def _():
        h = _rms_rows(x_ref[...], nw_ref[...])
        h_s[...] = h.astype(BF16)
        acc_s[...] = jnp.zeros_like(acc_s)
        if n_experts > 1:
            r_hi, r_lo = rt_ref[0], rt_ref[1]
            h_hi, h_mid, h_lo = _split3(h)
            d = lambda a, b: jnp.dot(a, b, preferred_element_type=F32)
            logits = d(h_hi, r_hi) + d(h_mid, r_hi) + d(h_hi, r_lo) + d(h_lo, r_hi) + d(h_mid, r_lo)
            lane = lax.broadcasted_iota(jnp.int32, logits.shape, 1)
            neg = jnp.float32(-jnp.inf)
            logits = jnp.where(lane < n_experts, logits, neg)
            m1 = jnp.max(logits, axis=-1, keepdims=True)
            i1 = jnp.min(jnp.where(logits == m1, lane, LANES), axis=-1, keepdims=True)
            rest = jnp.where(lane == i1, neg, logits)
            m2 = jnp.max(rest, axis=-1, keepdims=True)
            i2 = jnp.min(jnp.where(rest == m2, lane, LANES), axis=-1, keepdims=True)
            e2 = jnp.exp(m2 - m1)
            w1 = 1.0 / (1.0 + e2)
            gate_s[...] = jnp.where(lane == i1, w1, 0.0) + jnp.where(lane == i2, e2 * w1, 0.0)

    h = h_s[...]
    act = _silu(jnp.dot(h, wg_ref[0], preferred_element_type=F32)) * jnp.dot(
        h, wu_ref[0], preferred_element_type=F32)
    if n_experts > 1:
        gates = gate_s[...]
        lane = lax.broadcasted_iota(jnp.int32, gates.shape, 1)
        act = act * jnp.sum(jnp.where(lane == e, gates, 0.0), axis=-1, keepdims=True)
    acc_s[...] += _dot(act, wd_ref[0])

    @pl.when((e == n_experts - 1) & (j == pl.num_programs(2) - 1))
    def _():
        y = x_ref[...] + acc_s[...]
        if final_norm:
            y = _rms_rows(y, fw_ref[...])
        o_ref[...] = y


def _channel_mix(x2, norm_w, wg, wu, wd, router, final_w, tm, tf):
    n, d = x2.shape
    n_experts, _, f = wg.shape
    final_norm = final_w is not None
    fw = (final_w if final_norm else jnp.ones((d,), F32)).reshape(1, d)
    ins = [x2, norm_w.reshape(1, d)]
    in_specs = [pl.BlockSpec((tm, d), lambda i, e, j: (i, 0)), pl.BlockSpec((1, d), lambda i, e, j: (0, 0))]
    scratch = [pltpu.VMEM((tm, d), BF16), pltpu.VMEM((tm, d), F32)]
    if n_experts > 1:
        rt = jnp.pad(router.astype(F32), ((0, 0), (0, LANES - n_experts)))
        r_hi = rt.astype(BF16)
        r_lo = (rt - r_hi.astype(F32)).astype(BF16)
        ins.append(jnp.stack([r_hi, r_lo]))
        in_specs.append(pl.BlockSpec((2, d, LANES), lambda i, e, j: (0, 0, 0)))
        scratch.append(pltpu.VMEM((tm, LANES), F32))
    ins += [wg, wu, wd, fw]
    in_specs += [pl.BlockSpec((1, d, tf), lambda i, e, j: (e, 0, j)),
                 pl.BlockSpec((1, d, tf), lambda i, e, j: (e, 0, j)),
                 pl.BlockSpec((1, tf, d), lambda i, e, j: (e, j, 0)),
                 pl.BlockSpec((1, d), lambda i, e, j: (0, 0))]
    return pl.pallas_call(
        functools.partial(_ffn_kernel, n_experts=n_experts, final_norm=final_norm),
        grid=(n // tm, n_experts, f // tf),
        in_specs=in_specs,
        out_specs=pl.BlockSpec((tm, d), lambda i, e, j: (i, 0)),
        out_shape=jax.ShapeDtypeStruct((n, d), F32),
        scratch_shapes=scratch,
        compiler_params=_cparams("parallel", "arbitrary", "arbitrary"),
        name="channel_mix",
    )(*ins)


def _tiles(n_rows, t):
    tm = 512 if n_rows % 512 == 0 else 256
    tt = 256 if t % 256 == 0 else RW_CHUNK
    return dict(tm=tm, tt=tt)


def _ff_tile(f):
    for parts in (2, 4, 7, 11, 14, 22, 28):
        if f % parts == 0 and (f // parts) % LANES == 0 and f // parts <= 2048:
            return f // parts
    return f


def _deinterleave_heads(w):
    d_in = w.shape[0]
    return w.reshape(d_in, HEADS, HEAD_DIM // 2, 2).transpose(0, 1, 3, 2).reshape(d_in, WIDTH)


def kernel(x, norm1_w, w_in, rw_mu, rw_w_up, rw_w0, rw_a_up, rw_a0, rw_g_up, rw_k_k, rw_k_a, rw_r_k, rw_gn_w, rw_gn_b, rw_v_down, rw_v_up, rw_v0, ssm_conv_w, ssm_conv_b, ssm_dt_bias, ssm_a_log, ssm_d, ssm_norm_w, w_branch, w_out, norm2_w, ffn_wg, ffn_wu, ffn_wd, moe_router, moe_wg, moe_wu, moe_wd, final_norm_w):
    bsz, t, d = x.shape
    depth = w_in.shape[0]
    n = bsz * t
    W = WIDTH
    tl = _tiles(n, t)
    rw_cols = 3 * W + rw_w_up.shape[1] + rw_a_up.shape[1] + rw_g_up.shape[1]
    cdim = ssm_conv_w.shape[1]
    ssm_cols = W + cdim + ssm_a_log.shape[1]
    c1 = rw_cols
    c2 = c1 + ssm_cols
    c3 = c2 + 4 * W
    ssm_pad = (-ssm_cols) % LANES

    x2 = x.reshape(n, d)
    v_first = None
    for layer in range(depth):
        wl = w_in[layer]
        w_rw = wl[:, :c1].astype(BF16)
        w_ssm = jnp.pad(wl[:, c1:c2], ((0, 0), (0, ssm_pad))).astype(BF16)
        w_ret = jnp.concatenate([_deinterleave_heads(wl[:, c2:c2 + W]),
                                 _deinterleave_heads(wl[:, c2 + W:c2 + 2 * W]),
                                 wl[:, c2 + 2 * W:c3]], axis=1).astype(BF16)
        w_gate = wl[:, c3:].astype(BF16)
        p_rw = _norm_proj(x2, norm1_w[layer], w_rw, tl['tm']).reshape(bsz, t, -1)
        p_ssm = _norm_proj(x2, norm1_w[layer], w_ssm, tl['tm']).reshape(bsz, t, -1)
        p_ret = _norm_proj(x2, norm1_w[layer], w_ret, tl['tm']).reshape(bsz, t, -1)
        gate_logits = _norm_proj(x2, norm1_w[layer], w_gate, tl['tm'])

        rw_prm = dict(mu=rw_mu[layer], w_up=rw_w_up[layer], w0=rw_w0[layer], a_up=rw_a_up[layer],
                      a0=rw_a0[layer], g_up=rw_g_up[layer], k_k=rw_k_k[layer], k_a=rw_k_a[layer],
                      r_k=rw_r_k[layer].reshape(-1), gn_w=rw_gn_w[layer], gn_b=rw_gn_b[layer])
        vres = None if layer == 0 else (rw_v_down[layer - 1], rw_v_up[layer - 1], rw_v0[layer - 1])
        y_rw, v_first = _rwkv_mix(p_rw, rw_prm, v_first, vres, tl['tt'])
        ssm_prm = dict(conv_w=ssm_conv_w[layer], conv_b=ssm_conv_b[layer], dt_bias=ssm_dt_bias[layer],
                       a_log=ssm_a_log[layer], d=ssm_d[layer], norm_w=ssm_norm_w[layer])
        y_ssm = _ssm_mix(p_ssm, ssm_prm)
        y_ret = _ret_mix(p_ret)
        ys = [y.reshape(n, W) for y in (y_rw, y_ssm, y_ret)]
        x2 = _merge(x2, gate_logits, ys, w_branch[layer], w_out[layer], tl['tm'])

        j = layer // 2
        final_w = final_norm_w if layer == depth - 1 else None
        if layer % 2 == 0:
            x2 = _channel_mix(x2, norm2_w[layer], ffn_wg[j][None].astype(BF16), ffn_wu[j][None].astype(BF16),
                              ffn_wd[j][None].astype(BF16), None, final_w, tl['tm'], _ff_tile(ffn_wg.shape[2]))
        else:
            x2 = _channel_mix(x2, norm2_w[layer], moe_wg[j].astype(BF16), moe_wu[j].astype(BF16),
                              moe_wd[j].astype(BF16), moe_router[j], final_w, tl['tm'],
                              _ff_tile(moe_wg.shape[3]))
    return x2.reshape(bsz, t, d)
```

```python
import functools
import math

import jax
import jax.numpy as jnp
from jax import lax
from jax.experimental import pallas as pl
from jax.experimental.pallas import tpu as pltpu

F32 = jnp.float32
BF16 = jnp.bfloat16

LANES = 128
SUBLANES = 8
VMEM_LIMIT_BYTES = 56 * 1024 * 1024

RMS_EPS = 1e-6
RW_GN_EPS = 64e-5
HEADS = 8
HEAD_DIM = 64
WIDTH = HEADS * HEAD_DIM
RW_LORA_PAD = 128
RW_MV_LORA = 32
SSM_GROUPS = 2
SSM_STATE = 128
SSM_CONV = 4
ROPE_BASE = 10000.0
N_EXPERTS = 8

RW_CHUNK = 64
SSM_CHUNK = 128
RET_CHUNK = 128


def _cparams(*sem):
    return pltpu.CompilerParams(dimension_semantics=sem, vmem_limit_bytes=VMEM_LIMIT_BYTES)


def _dot(a, b):
    return jnp.dot(a.astype(BF16), b.astype(BF16), preferred_element_type=F32)


def _dot_nt(a, b):
    return lax.dot_general(a.astype(BF16), b.astype(BF16), (((1,), (1,)), ((), ())),
                           preferred_element_type=F32)


def _dot_tn(a, b):
    return lax.dot_general(a.astype(BF16), b.astype(BF16), (((0,), (0,)), ((), ())),
                           preferred_element_type=F32)


def _bmm(a, b):
    return lax.dot_general(a.astype(BF16), b.astype(BF16), (((2,), (1,)), ((0,), (0,))),
                           preferred_element_type=F32)


def _bmm_nt(a, b):
    return lax.dot_general(a.astype(BF16), b.astype(BF16), (((2,), (2,)), ((0,), (0,))),
                           preferred_element_type=F32)


def _bmm_tn(a, b):
    return lax.dot_general(a.astype(BF16), b.astype(BF16), (((1,), (1,)), ((0,), (0,))),
                           preferred_element_type=F32)


def _split2(x):
    hi = x.astype(BF16)
    lo = (x - hi.astype(F32)).astype(BF16)
    return hi, lo


def _split3(x):
    hi = x.astype(BF16)
    r = x - hi.astype(F32)
    mid = r.astype(BF16)
    lo = (r - mid.astype(F32)).astype(BF16)
    return hi, mid, lo


def _dot_exact_rhs(x, m_bf16):
    hi, mid, lo = _split3(x)
    d = lambda a: jnp.dot(a, m_bf16, preferred_element_type=F32)
    return d(hi) + d(mid) + d(lo)


def _dot_exact_lhs(m_bf16, x):
    hi, mid, lo = _split3(x)
    d = lambda a: jnp.dot(m_bf16, a, preferred_element_type=F32)
    return d(hi) + d(mid) + d(lo)


def _sigmoid(x):
    return 1.0 / (1.0 + jnp.exp(-x))


def _silu(x):
    return x * _sigmoid(x)


def _softplus(x):
    return jnp.maximum(x, 0.0) + jnp.log(1.0 + jnp.exp(-jnp.abs(x)))


def _rms_rows(x, w):
    return x * lax.rsqrt(jnp.mean(x * x, axis=-1, keepdims=True) + RMS_EPS) * w


def _norm_proj_kernel(x_ref, nw_ref, w_ref, o_ref):
    h = _rms_rows(x_ref[...], nw_ref[...]).astype(BF16)
    o_ref[...] = jnp.dot(h, w_ref[...], preferred_element_type=F32)


def _norm_proj(x2, norm_w, w_bf16, tm):
    n, d = x2.shape
    c = w_bf16.shape[1]
    return pl.pallas_call(
        _norm_proj_kernel,
        grid=(n // tm,),
        in_specs=[pl.BlockSpec((tm, d), lambda i: (i, 0)),
                  pl.BlockSpec((1, d), lambda i: (0, 0)),
                  pl.BlockSpec((d, c), lambda i: (0, 0))],
        out_specs=pl.BlockSpec((tm, c), lambda i: (i, 0)),
        out_shape=jax.ShapeDtypeStruct((n, c), F32),
        compiler_params=_cparams("parallel"),
        name="norm_proj",
    )(x2, norm_w.reshape(1, d), w_bf16)


def _head_sum(x, blk_ref):
    return _dot_exact_rhs(x, blk_ref[...])


def _rwkv_a_kernel(*refs, tt, has_vres):
    if has_vres:
        (p_ref, pprev_ref, mu_ref, wup_ref, w0_ref, aup_ref, a0_ref, gup_ref, kk_ref, ka_ref,
         rk_ref, blk_ref, vfirst_ref, vdown_ref, vup_ref, v0_ref,
         ry_ref, mc_ref, bonus_ref, g_ref) = refs
        vout_ref = None
    else:
        (p_ref, pprev_ref, mu_ref, wup_ref, w0_ref, aup_ref, a0_ref, gup_ref, kk_ref, ka_ref,
         rk_ref, blk_ref,
         ry_ref, mc_ref, bonus_ref, g_ref, vout_ref) = refs
    L = RW_CHUNK
    nchunk = tt // L
    W = WIDTH

    p = p_ref[0]
    prev = jnp.where(pl.program_id(1) == 0, 0.0, pprev_ref[0][SUBLANES - 1:SUBLANES, :])
    row = lax.broadcasted_iota(jnp.int32, (tt, 1), 0)
    shifted = jnp.where(row == 0, prev, pltpu.roll(p, 1, axis=0))
    xm = p + (shifted - p) * mu_ref[...]

    r = xm[:, 0:W]
    k = xm[:, W:2 * W]
    v = xm[:, 2 * W:3 * W]
    xwa = xm[:, 3 * W:3 * W + RW_LORA_PAD]
    xg = xm[:, 3 * W + RW_LORA_PAD:]

    ld = -math.exp(-0.5) * _sigmoid(w0_ref[...] + _dot(jnp.tanh(xwa), wup_ref[...]))
    a_sig = _sigmoid(a0_ref[...] + _dot(xwa, aup_ref[...]))
    g_ref[0] = _dot(_sigmoid(xg), gup_ref[...])
    if has_vres:
        vu = _dot(_dot(v, vdown_ref[...]), vup_ref[...])
        v = v + (vfirst_ref[0] - v) * _sigmoid(v0_ref[...] + vu)
    else:
        vout_ref[0] = v

    kk = k * kk_ref[...]
    kk = kk * lax.rsqrt(jnp.maximum(_head_sum(kk * kk, blk_ref), 1e-24))
    k = k * (1.0 + (a_sig - 1.0) * ka_ref[...])
    bonus_ref[0] = _head_sum(r * k * rk_ref[...], blk_ref) * v
    avec = -kk
    bvec = kk * a_sig

    cum = ld
    pos = row & (L - 1)
    d = 1
    while d < L:
        cum = cum + jnp.where(pos >= d, pltpu.roll(cum, d, axis=0), 0.0)
        d *= 2
    cum3 = cum.reshape(nchunk, L, W)
    last = jnp.broadcast_to(cum3[:, L - 1:L, :], (nchunk, L, W)).reshape(tt, W)
    w_inv = jnp.exp(-cum)
    w_last = jnp.exp(last - cum)

    def to_g(val):
        val = val.astype(BF16)
        return jnp.concatenate(
            [val[:, h * HEAD_DIM:(h + 1) * HEAD_DIM].reshape(nchunk, L, HEAD_DIM) for h in range(HEADS)],
            axis=0)

    at = to_g(avec * jnp.exp(cum - ld))
    rt = to_g(r * jnp.exp(cum))
    bt = to_g(bvec * w_inv)
    kt = to_g(k * w_inv)
    bh = to_g(bvec * w_last)
    kh = to_g(k * w_last)
    vv = to_g(v)
    wl = jnp.concatenate(
        [jnp.exp(cum3[:, L - 1:L, h * HEAD_DIM:(h + 1) * HEAD_DIM]) for h in range(HEADS)], axis=0)

    ri = lax.broadcasted_iota(jnp.int32, (1, L, L), 1)
    ci = lax.broadcasted_iota(jnp.int32, (1, L, L), 2)
    strict = ri > ci
    incl = ri >= ci
    eye = ri == ci

    g4 = _bmm_nt(jnp.concatenate([at, rt], axis=1), jnp.concatenate([bt, kt], axis=1))
    nmat = jnp.where(strict, g4[:, 0:L, 0:L], 0.0)
    a_ak = jnp.where(strict, g4[:, 0:L, L:2 * L], 0.0)
    a_rb = jnp.where(incl, g4[:, L:2 * L, 0:L], 0.0)
    a_rk = jnp.where(incl, g4[:, L:2 * L, L:2 * L], 0.0)
    tp = nmat
    npow = nmat
    span = 1
    while span * 2 < L:
        npow = _bmm(npow, npow)
        tp = tp + npow + _bmm(tp, npow)
        span *= 2
    av = _bmm(jnp.concatenate([a_ak, a_rk], axis=1), vv)
    x0 = jnp.concatenate([at.astype(F32), av[:, 0:L]], axis=2)
    x = x0 + _bmm(tp, x0)
    ry = jnp.concatenate([rt.astype(F32), av[:, L:2 * L]], axis=2) + _bmm(a_rb, x)
    bx = _bmm_tn(bh, x)
    kv = _bmm_tn(kh, vv)
    mc = bx + jnp.concatenate([jnp.where(eye, wl, 0.0), kv], axis=2)
    ry_ref[0] = ry.reshape(HEADS, tt, 2 * HEAD_DIM)
    mc_ref[0] = mc.reshape(HEADS, tt, 2 * HEAD_DIM)


def _rwkv_b_kernel(ry_ref, mc_ref, bonus_ref, g_ref, gnw_ref, gnb_ref, blk_ref, o_ref, st_s, y_s, *, tt):
    L = RW_CHUNK
    nchunk = tt // L

    @pl.when(pl.program_id(1) == 0)
    def _():
        st_s[...] = jnp.zeros_like(st_s)

    def chunk_body(c, carry):
        r0 = pl.multiple_of(c * L, L)
        for h in range(HEADS):
            st = st_s[h]
            ry = ry_ref[0, h, pl.ds(r0, L), :]
            mc = mc_ref[0, h, pl.ds(r0, L), :]
            y = ry[:, HEAD_DIM:] + _dot(ry[:, :HEAD_DIM], st)
            m_hi, m_lo = _split2(mc[:, :HEAD_DIM])
            s_hi, s_lo = _split2(st)
            d = lambda a, b: jnp.dot(a, b, preferred_element_type=F32)
            st_s[h] = mc[:, HEAD_DIM:] + d(m_hi, s_hi) + d(m_lo, s_hi) + d(m_hi, s_lo)
            y_s[pl.ds(r0, L), h * HEAD_DIM:(h + 1) * HEAD_DIM] = y
        return carry

    lax.fori_loop(0, nchunk, chunk_body, 0)

    y = y_s[...]
    inv = 1.0 / HEAD_DIM
    mean = _head_sum(y, blk_ref) * inv
    dlt = y - mean
    var = _head_sum(dlt * dlt, blk_ref) * inv
    yn = dlt * lax.rsqrt(var + RW_GN_EPS) * gnw_ref[...] + gnb_ref[...]
    o_ref[0] = ((yn + bonus_ref[0]) * g_ref[0]).astype(o_ref.dtype)


def _rwkv_mix(p_rw, prm, v_first, vres, tt):
    bsz, t, cols = p_rw.shape
    W = WIDTH
    has_vres = vres is not None
    row = lambda a: a.reshape(1, -1).astype(F32)
    full = lambda shape: pl.BlockSpec(shape, lambda b, i: (0,) * len(shape))
    tile = lambda c: pl.BlockSpec((1, tt, c), lambda b, i: (b, i, 0))
    blk = (jnp.arange(W)[:, None] // HEAD_DIM == jnp.arange(W)[None, :] // HEAD_DIM).astype(BF16)

    zpad = jnp.zeros((RW_LORA_PAD // 2, W), F32)
    wup = jnp.concatenate([prm['w_up'], zpad], axis=0).astype(BF16)
    aup = jnp.concatenate([zpad, prm['a_up']], axis=0).astype(BF16)
    ins = [p_rw, p_rw, row(prm['mu']), wup, row(prm['w0']), aup, row(prm['a0']),
           prm['g_up'].astype(BF16), row(prm['k_k']), row(prm['k_a']), row(prm['r_k']), blk]
    in_specs = [tile(cols),
                pl.BlockSpec((1, SUBLANES, cols),
                             lambda b, i: (b, jnp.maximum(i * (tt // SUBLANES) - 1, 0), 0)),
                full((1, cols)), full((RW_LORA_PAD, W)), full((1, W)), full((RW_LORA_PAD, W)),
                full((1, W)), full((RW_LORA_PAD, W)), full((1, W)), full((1, W)), full((1, W)),
                full((W, W))]
    if has_vres:
        v_down, v_up, v0 = vres
        pad = LANES - RW_MV_LORA
        ins += [v_first, jnp.pad(v_down, ((0, 0), (0, pad))).astype(BF16),
                jnp.pad(v_up, ((0, pad), (0, 0))).astype(BF16), row(v0)]
        in_specs += [tile(W), full((W, LANES)), full((LANES, W)), full((1, W))]

    hm = jax.ShapeDtypeStruct((bsz, HEADS, t, 2 * HEAD_DIM), F32)
    tw = jax.ShapeDtypeStruct((bsz, t, W), F32)
    hm_spec = pl.BlockSpec((1, HEADS, tt, 2 * HEAD_DIM), lambda b, i: (b, 0, i, 0))
    out_shape = [hm, hm, tw, tw]
    out_specs = [hm_spec, hm_spec, tile(W), tile(W)]
    if not has_vres:
        out_shape.append(tw)
        out_specs.append(tile(W))
    outs = pl.pallas_call(
        functools.partial(_rwkv_a_kernel, tt=tt, has_vres=has_vres),
        grid=(bsz, t // tt),
        in_specs=in_specs, out_specs=out_specs, out_shape=out_shape,
        compiler_params=_cparams("parallel", "parallel"),
        name="rwkv_a",
    )(*ins)
    if has_vres:
        ry, mc, bonus, g = outs
    else:
        ry, mc, bonus, g, v_first = outs

    y = pl.pallas_call(
        functools.partial(_rwkv_b_kernel, tt=tt),
        grid=(bsz, t // tt),
        in_specs=[hm_spec, hm_spec, tile(W), tile(W), full((1, W)), full((1, W)), full((W, W))],
        out_specs=tile(W),
        out_shape=jax.ShapeDtypeStruct((bsz, t, W), BF16),
        scratch_shapes=[pltpu.VMEM((HEADS, HEAD_DIM, HEAD_DIM), F32), pltpu.VMEM((tt, W), F32)],
        compiler_params=_cparams("parallel", "arbitrary"),
        name="rwkv_b",
    )(ry, mc, bonus, g, row(prm['gn_w']), row(prm['gn_b']), blk)
    return y, v_first


def _ssm_kernel(p_ref, pprev_ref, cw_ref, cb_ref, dtb_ref, alog_ref, dsk_ref, nw_ref, exp_ref,
                tri_ref, o_ref, st_s):
    Q = SSM_CHUNK
    W = WIDTH
    nb = SSM_GROUPS * SSM_STATE
    gw = W // SSM_GROUPS

    @pl.when(pl.program_id(1) == 0)
    def _():
        st_s[...] = jnp.zeros_like(st_s)

    p = p_ref[0]
    z = p[:, 0:W]
    xbc_in = p[:, W:2 * W + 2 * nb]
    dt_in = p[:, 2 * W + 2 * nb:]
    prev = jnp.where(pl.program_id(1) == 0, 0.0, pprev_ref[0][:, W:2 * W + 2 * nb])
    row = lax.broadcasted_iota(jnp.int32, (Q, 1), 0)
    acc = cb_ref[...] + cw_ref[SSM_CONV - 1:SSM_CONV, :] * xbc_in
    for j in range(1, SSM_CONV):
        sh = pltpu.roll(xbc_in, j, axis=0)
        for i in range(j):
            sh = jnp.where(row == i, prev[SUBLANES - j + i:SUBLANES - j + i + 1, :], sh)
        acc = acc + cw_ref[SSM_CONV - 1 - j:SSM_CONV - j, :] * sh
    xbc = _silu(acc)
    xs = xbc[:, 0:W]
    bm = xbc[:, W:W + nb]
    cm = xbc[:, W + nb:]

    dt = _softplus(dt_in + dtb_ref[...])
    a = -jnp.exp(alog_ref[...]) * dt
    a_cs = _dot_exact_lhs(tri_ref[...], a)
    a_cs_t = a_cs.T
    dt_e = _dot_exact_rhs(dt, exp_ref[...])
    acs_e = _dot_exact_rhs(a_cs, exp_ref[...])
    last_e = acs_e[Q - 1:Q, :]
    xdt = xs * dt_e
    xdec = xdt * jnp.exp(last_e - acs_e)

    li = lax.broadcasted_iota(jnp.int32, (Q, Q), 0)
    si = lax.broadcasted_iota(jnp.int32, (Q, Q), 1)
    causal = li >= si
    hg = HEADS // SSM_GROUPS
    ys = []
    for g in range(SSM_GROUPS):
        bg = bm[:, g * SSM_STATE:(g + 1) * SSM_STATE]
        cg = cm[:, g * SSM_STATE:(g + 1) * SSM_STATE]
        cb = _dot_nt(cg, bg)
        for hh in range(hg):
            h = g * hg + hh
            seg = a_cs[:, h:h + 1] - a_cs_t[h:h + 1, :]
            lmat = jnp.where(causal, jnp.exp(jnp.where(causal, seg, 0.0)), 0.0)
            ys.append(_dot(cb * lmat, xdt[:, h * HEAD_DIM:(h + 1) * HEAD_DIM]))
        st = st_s[:, g * gw:(g + 1) * gw]
        y_off = _dot(cg, st) * jnp.exp(acs_e[:, g * gw:(g + 1) * gw])
        ys.append(y_off)
        st_s[:, g * gw:(g + 1) * gw] = (jnp.exp(last_e[:, g * gw:(g + 1) * gw]) * st
                                        + _dot_tn(bg, xdec[:, g * gw:(g + 1) * gw]))
    n5 = hg + 1
    y = jnp.concatenate(
        [jnp.concatenate(ys[g * n5:g * n5 + hg], axis=1) + ys[g * n5 + hg] for g in range(SSM_GROUPS)],
        axis=1)
    y = (y + dsk_ref[...] * xs) * _silu(z)
    outs = []
    for g in range(SSM_GROUPS):
        yg = y[:, g * gw:(g + 1) * gw]
        outs.append(yg * lax.rsqrt(jnp.mean(yg * yg, axis=-1, keepdims=True) + RMS_EPS))
    o_ref[0] = (jnp.concatenate(outs, axis=1) * nw_ref[...]).astype(o_ref.dtype)


def _ssm_mix(p_ssm, prm):
    bsz, t, cols = p_ssm.shape
    W = WIDTH
    Q = SSM_CHUNK
    cdim = W + 2 * SSM_GROUPS * SSM_STATE
    row = lambda a: a.reshape(1, -1).astype(F32)
    padrow = lambda a: jnp.pad(a.astype(F32), (0, LANES - a.shape[0])).reshape(1, LANES)
    full = lambda shape: pl.BlockSpec(shape, lambda b, i: (0,) * len(shape))
    expand = (jnp.arange(LANES)[:, None] == jnp.arange(W)[None, :] // HEAD_DIM).astype(BF16)
    tri = (jnp.arange(Q)[:, None] >= jnp.arange(Q)[None, :]).astype(BF16)
    return pl.pallas_call(
        _ssm_kernel,
        grid=(bsz, t // Q),
        in_specs=[pl.BlockSpec((1, Q, cols), lambda b, i: (b, i, 0)),
                  pl.BlockSpec((1, SUBLANES, cols),
                               lambda b, i: (b, jnp.maximum(i * (Q // SUBLANES) - 1, 0), 0)),
                  full((SSM_CONV, cdim)), full((1, cdim)), full((1, LANES)), full((1, LANES)),
                  full((1, W)), full((1, W)), full((LANES, W)), full((Q, Q))],
        out_specs=pl.BlockSpec((1, Q, W), lambda b, i: (b, i, 0)),
        out_shape=jax.ShapeDtypeStruct((bsz, t, W), BF16),
        scratch_shapes=[pltpu.VMEM((SSM_STATE, W), F32)],
        compiler_params=_cparams("parallel", "arbitrary"),
        name="ssm",
    )(p_ssm, p_ssm, prm['conv_w'].T.astype(F32), row(prm['conv_b']), padrow(prm['dt_bias']),
      padrow(prm['a_log']), row(jnp.repeat(prm['d'], HEAD_DIM)), row(prm['norm_w']), expand, tri)


def _ret_log_gamma(h):
    return math.log1p(-(2.0 ** (-5.0 - h)))


def _ret_kernel(p_ref, freq_ref, sgn_ref, blk_ref, o_ref, st_s, dec_s, y_s):
    Q = RET_CHUNK
    W = WIDTH
    half = HEAD_DIM // 2

    @pl.when(pl.program_id(1) == 0)
    def _():
        st_s[...] = jnp.zeros_like(st_s)
        li = lax.broadcasted_iota(jnp.int32, (Q, Q), 0)
        si = lax.broadcasted_iota(jnp.int32, (Q, Q), 1)
        rel = (li - si).astype(F32)
        for h in range(HEADS):
            dec_s[h] = jnp.where(li >= si, jnp.exp(jnp.where(li >= si, rel, 0.0) * _ret_log_gamma(h)), 0.0)

    p = p_ref[0]
    idx = lax.broadcasted_iota(jnp.int32, (Q, 1), 0)
    pos = (pl.program_id(1) * Q + idx).astype(F32)
    ang = pos * freq_ref[...]
    cos = jnp.concatenate([jnp.cos(ang)] * (W // LANES), axis=1)
    sin = jnp.concatenate([jnp.sin(ang) * sgn_ref[...]] * (W // LANES), axis=1)
    lane = lax.broadcasted_iota(jnp.int32, (1, W), 1)
    first_half = (lane & (HEAD_DIM - 1)) < half

    def rot(x):
        partner = jnp.where(first_half, pltpu.roll(x, W - half, axis=1), pltpu.roll(x, half, axis=1))
        return x * cos + partner * sin

    q = rot(p[:, 0:W])
    k = rot(p[:, W:2 * W]) * (HEAD_DIM ** -0.5)
    v = p[:, 2 * W:3 * W]
    g = p[:, 3 * W:]
    idf = idx.astype(F32)
    for h in range(HEADS):
        lg = _ret_log_gamma(h)
        sl = slice(h * HEAD_DIM, (h + 1) * HEAD_DIM)
        qh, kh, vh = q[:, sl], k[:, sl], v[:, sl]
        scores = _dot_nt(qh, kh) * dec_s[h]
        st = st_s[h]
        y = _dot(scores, vh) + _dot(qh, st) * jnp.exp((idf + 1.0) * lg)
        st_s[h] = math.exp(Q * lg) * st + _dot_tn(kh * jnp.exp((Q - 1.0 - idf) * lg), vh)
        y_s[:, sl] = y
    y = y_s[...]
    ms = _head_sum(y * y, blk_ref) * (1.0 / HEAD_DIM)
    o_ref[0] = (y * lax.rsqrt(ms + RMS_EPS) * _silu(g)).astype(o_ref.dtype)


def _ret_mix(p_ret):
    bsz, t, cols = p_ret.shape
    W = WIDTH
    Q = RET_CHUNK
    half = HEAD_DIM // 2
    full = lambda shape: pl.BlockSpec(shape, lambda b, i: (0,) * len(shape))
    inv_freq = ROPE_BASE ** (-jnp.arange(half, dtype=F32) / half)
    freq = jnp.tile(inv_freq, LANES // half).reshape(1, LANES)
    sgn = jnp.where((jnp.arange(LANES) % HEAD_DIM) < half, -1.0, 1.0).astype(F32).reshape(1, LANES)
    blk = (jnp.arange(W)[:, None] // HEAD_DIM == jnp.arange(W)[None, :] // HEAD_DIM).astype(BF16)
    return pl.pallas_call(
        _ret_kernel,
        grid=(bsz, t // Q),
        in_specs=[pl.BlockSpec((1, Q, cols), lambda b, i: (b, i, 0)),
                  full((1, LANES)), full((1, LANES)), full((W, W))],
        out_specs=pl.BlockSpec((1, Q, W), lambda b, i: (b, i, 0)),
        out_shape=jax.ShapeDtypeStruct((bsz, t, W), BF16),
        scratch_shapes=[pltpu.VMEM((HEADS, HEAD_DIM, HEAD_DIM), F32), pltpu.VMEM((HEADS, Q, Q), F32),
                        pltpu.VMEM((Q, W), F32)],
        compiler_params=_cparams("parallel", "arbitrary"),
        name="retention",
    )(p_ret, freq, sgn, blk)


def _merge_kernel(x_ref, gl_ref, y0_ref, y1_ref, y2_ref, wb_ref, wo_ref, o_ref):
    d = x_ref.shape[1]
    merged = None
    for i, y_ref in enumerate((y0_ref, y1_ref, y2_ref)):
        gate = _sigmoid(gl_ref[:, i * d:(i + 1) * d])
        term = gate * jnp.dot(y_ref[...], wb_ref[i], preferred_element_type=F32)
        merged = term if merged is None else merged + term
    o_ref[...] = x_ref[...] + _dot(merged, wo_ref[...])


def _merge(x2, gate_logits, ys, w_branch, w_out, tm):
    n, d = x2.shape
    W = WIDTH
    rows = lambda c: pl.BlockSpec((tm, c), lambda i: (i, 0))
    return pl.pallas_call(
        _merge_kernel,
        grid=(n // tm,),
        in_specs=[rows(d), rows(3 * d), rows(W), rows(W), rows(W),
                  pl.BlockSpec((3, W, d), lambda i: (0, 0, 0)),
                  pl.BlockSpec((d, d), lambda i: (0, 0))],
        out_specs=rows(d),
        out_shape=jax.ShapeDtypeStruct((n, d), F32),
        compiler_params=_cparams("parallel"),
        name="merge_out",
    )(x2, gate_logits, *ys, w_branch.astype(BF16), w_out.astype(BF16))


def _ffn_kernel(x_ref, nw_ref, wg_ref, wu_ref, wd_ref, fw_ref, o_ref, h_s, acc_s, *, final_norm):
    j = pl.program_id(1)

    @pl.when(j == 0)
    def _():
        h_s[...] = _rms_rows(x_ref[...], nw_ref[...]).astype(BF16)
        acc_s[...] = jnp.zeros_like(acc_s)

    h = h_s[...]
    act = _silu(jnp.dot(h, wg_ref[...], preferred_element_type=F32)) * jnp.dot(
        h, wu_ref[...], preferred_element_type=F32)
    acc_s[...] += _dot(act, wd_ref[...])

    @pl.when(j == pl.num_programs(1) - 1)
    def _():
        y = x_ref[...] + acc_s[...]
        if final_norm:
            y = _rms_rows(y, fw_ref[...])
        o_ref[...] = y


def _ffn(x2, norm_w, wg, wu, wd, final_w, tm, tf):
    n, d = x2.shape
    f = wg.shape[1]
    final_norm = final_w is not None
    fw = (final_w if final_norm else jnp.ones((d,), F32)).reshape(1, d)
    return pl.pallas_call(
        functools.partial(_ffn_kernel, final_norm=final_norm),
        grid=(n // tm, f // tf),
        in_specs=[pl.BlockSpec((tm, d), lambda i, j: (i, 0)), pl.BlockSpec((1, d), lambda i, j: (0, 0)),
                  pl.BlockSpec((d, tf), lambda i, j: (0, j)), pl.BlockSpec((d, tf), lambda i, j: (0, j)),
                  pl.BlockSpec((tf, d), lambda i, j: (j, 0)), pl.BlockSpec((1, d), lambda i, j: (0, 0))],
        out_specs=pl.BlockSpec((tm, d), lambda i, j: (i, 0)),
        out_shape=jax.ShapeDtypeStruct((n, d), F32),
        scratch_shapes=[pltpu.VMEM((tm, d), BF16), pltpu.VMEM((tm, d), F32)],
        compiler_params=_cparams("parallel", "arbitrary"),
        name="ffn",
    )(x2, norm_w.reshape(1, d), wg, wu, wd, fw)


def _router_kernel(x_ref, nw_ref, rt_ref, h_ref, gate_ref, pos_ref, post_ref, cnt_ref, *, n_experts):
    tm = x_ref.shape[0]
    h = _rms_rows(x_ref[...], nw_ref[...])
    h_ref[...] = h.astype(BF16)
    r_hi, r_lo = rt_ref[0], rt_ref[1]
    h_hi, h_mid, h_lo = _split3(h)
    d = lambda a, b: jnp.dot(a, b, preferred_element_type=F32)
    logits = d(h_hi, r_hi) + d(h_mid, r_hi) + d(h_hi, r_lo) + d(h_lo, r_hi) + d(h_mid, r_lo)
    lane = lax.broadcasted_iota(jnp.int32, logits.shape, 1)
    neg = jnp.float32(-jnp.inf)
    logits = jnp.where(lane < n_experts, logits, neg)
    m1 = jnp.max(logits, axis=-1, keepdims=True)
    i1 = jnp.min(jnp.where(logits == m1, lane, LANES), axis=-1, keepdims=True)
    rest = jnp.where(lane == i1, neg, logits)
    m2 = jnp.max(rest, axis=-1, keepdims=True)
    i2 = jnp.min(jnp.where(rest == m2, lane, LANES), axis=-1, keepdims=True)
    e2 = jnp.exp(m2 - m1)
    w1 = 1.0 / (1.0 + e2)
    gate_ref[...] = jnp.where(lane == i1, w1, 0.0) + jnp.where(lane == i2, e2 * w1, 0.0)
    member = (lane == i1) | (lane == i2)
    m = jnp.where(member, 1.0, 0.0)
    row = lax.broadcasted_iota(jnp.int32, (tm, 1), 0)
    c = m
    step = 1
    while step < tm:
        c = c + jnp.where(row >= step, pltpu.roll(c, step, axis=0), 0.0)
        step *= 2
    posm = jnp.where(member, c - m, -1.0)
    pos_ref[...] = posm
    post_ref[0] = posm.T[0:SUBLANES, :]
    cnt_ref[0] = c[tm - 1:tm, :].astype(jnp.int32)


def _moe_kernel(cnt_ref, x_ref, h_ref, gate_ref, pos_ref, post_ref, wg_ref, wu_ref, wd_ref, fw_ref, o_ref,
                acc_s, xg_s, yacc_s, *, n_experts, cap, final_norm):
    tm = x_ref.shape[0]
    i = pl.program_id(0)
    e = pl.program_id(1)
    j = pl.program_id(2)
    nj = pl.num_programs(2)

    @pl.when((e == 0) & (j == 0))
    def _():
        acc_s[...] = jnp.zeros_like(acc_s)

    lane = lax.broadcasted_iota(jnp.int32, (tm, LANES), 1)
    pick = lambda ref: jnp.sum(jnp.where(lane == e, ref[...], 0.0), axis=-1, keepdims=True)
    pos_col = pick(pos_ref)
    gate_col = pick(gate_ref)
    pos_row = post_ref[0, pl.ds(e, 1), :]

    def sel(b):
        slot = (lax.broadcasted_iota(jnp.int32, (cap, 1), 0) + b * cap).astype(F32)
        return jnp.where(pos_row == slot, 1.0, 0.0).astype(BF16)

    def sel_t(b):
        slot = (lax.broadcasted_iota(jnp.int32, (1, cap), 1) + b * cap).astype(F32)
        return jnp.where(pos_col == slot, 1.0, 0.0).astype(BF16)

    def expert(xg):
        act = _silu(jnp.dot(xg, wg_ref[0], preferred_element_type=F32)) * jnp.dot(
            xg, wu_ref[0], preferred_element_type=F32)
        return _dot(act, wd_ref[0])

    @pl.when(j == 0)
    def _():
        xg_s[...] = jnp.dot(sel(0), h_ref[...], preferred_element_type=F32).astype(BF16)

    yb = expert(xg_s[...])

    @pl.when(j == 0)
    def _():
        yacc_s[...] = yb

    @pl.when(j > 0)
    def _():
        yacc_s[...] += yb

    @pl.when(j == nj - 1)
    def _():
        acc_s[...] += gate_col * _dot(sel_t(0), yacc_s[...])

    nblk = (cnt_ref[i * n_experts + e] + cap - 1) // cap

    def overflow(b, carry):
        xg = jnp.dot(sel(b), h_ref[...], preferred_element_type=F32).astype(BF16)
        acc_s[...] += gate_col * _dot(sel_t(b), expert(xg))
        return carry

    lax.fori_loop(1, nblk, overflow, 0)

    @pl.when((e == n_experts - 1) & (j == nj - 1))
    def _():
        y = x_ref[...] + acc_s[...]
        if final_norm:
            y = _rms_rows(y, fw_ref[...])
        o_ref[...] = y


def _moe(x2, norm_w, wg, wu, wd, router, final_w, tm, tf, cap):
    n, d = x2.shape
    n_experts, _, f = wg.shape
    nt = n // tm
    final_norm = final_w is not None
    fw = (final_w if final_norm else jnp.ones((d,), F32)).reshape(1, d)
    rt = jnp.pad(router.astype(F32), ((0, 0), (0, LANES - n_experts)))
    r_hi = rt.astype(BF16)
    r_lo = (rt - r_hi.astype(F32)).astype(BF16)
    rows = lambda c: pl.BlockSpec((tm, c), lambda i: (i, 0))
    h, gates, pos, pos_t, cnt = pl.pallas_call(
        functools.partial(_router_kernel, n_experts=n_experts),
        grid=(nt,),
        in_specs=[rows(d), pl.BlockSpec((1, d), lambda i: (0, 0)),
                  pl.BlockSpec((2, d, LANES), lambda i: (0, 0, 0))],
        out_specs=[rows(d), rows(LANES), rows(LANES),
                   pl.BlockSpec((1, SUBLANES, tm), lambda i: (i, 0, 0)),
                   pl.BlockSpec((1, 1, LANES), lambda i: (i, 0, 0))],
        out_shape=[jax.ShapeDtypeStruct((n, d), BF16), jax.ShapeDtypeStruct((n, LANES), F32),
                   jax.ShapeDtypeStruct((n, LANES), F32), jax.ShapeDtypeStruct((nt, SUBLANES, tm), F32),
                   jax.ShapeDtypeStruct((nt, 1, LANES), jnp.int32)],
        compiler_params=_cparams("parallel"),
        name="moe_router",
    )(x2, norm_w.reshape(1, d), jnp.stack([r_hi, r_lo]))
    counts = cnt[:, 0, :n_experts].reshape(-1)
    trow = lambda c: pl.BlockSpec((tm, c), lambda i, e, j, cnt: (i, 0))
    grid_spec = pltpu.PrefetchScalarGridSpec(
        num_scalar_prefetch=1,
        grid=(nt, n_experts, f // tf),
        in_specs=[trow(d), trow(d), trow(LANES), trow(LANES),
                  pl.BlockSpec((1, SUBLANES, tm), lambda i, e, j, cnt: (i, 0, 0)),
                  pl.BlockSpec((1, d, tf), lambda i, e, j, cnt: (e, 0, j)),
                  pl.BlockSpec((1, d, tf), lambda i, e, j, cnt: (e, 0, j)),
                  pl.BlockSpec((1, tf, d), lambda i, e, j, cnt: (e, j, 0)),
                  pl.BlockSpec((1, d), lambda i, e, j, cnt: (0, 0))],
        out_specs=trow(d),
        scratch_shapes=[pltpu.VMEM((tm, d), F32), pltpu.VMEM((cap, d), BF16), pltpu.VMEM((cap, d), F32)],
    )
    return pl.pallas_call(
        functools.partial(_moe_kernel, n_experts=n_experts, cap=cap, final_norm=final_norm),
        grid_spec=grid_spec,
        out_shape=jax.ShapeDtypeStruct((n, d), F32),
        compiler_params=_cparams("parallel", "arbitrary", "arbitrary"),
        name="moe_experts",
    )(counts, x2, h, gates, pos, pos_t, wg, wu, wd, fw)


def _tiles(n_rows, t):
    tm = 512 if n_rows % 512 == 0 else 256
    tt = 256 if t % 256 == 0 else RW_CHUNK
    tm_moe = 1024 if n_rows % 1024 == 0 else tm
    cap = (tm_moe // 4 + tm_moe // 16 + 15) // 16 * 16
    return dict(tm=tm, tt=tt, tm_moe=tm_moe, cap=cap)


def _ff_tile(f, max_tile=2048):
    for parts in (2, 4, 7, 11, 14, 22, 28):
        if f % parts == 0 and (f // parts) % LANES == 0 and f // parts <= max_tile:
            return f // parts
    return f


def _deinterleave_heads(w):
    d_in = w.shape[0]
    return w.reshape(d_in, HEADS, HEAD_DIM // 2, 2).transpose(0, 1, 3, 2).reshape(d_in, WIDTH)


def kernel(x, norm1_w, w_in, rw_mu, rw_w_up, rw_w0, rw_a_up, rw_a0, rw_g_up, rw_k_k, rw_k_a, rw_r_k, rw_gn_w, rw_gn_b, rw_v_down, rw_v_up, rw_v0, ssm_conv_w, ssm_conv_b, ssm_dt_bias, ssm_a_log, ssm_d, ssm_norm_w, w_branch, w_out, norm2_w, ffn_wg, ffn_wu, ffn_wd, moe_router, moe_wg, moe_wu, moe_wd, final_norm_w):
    bsz, t, d = x.shape
    depth = w_in.shape[0]
    n = bsz * t
    W = WIDTH
    tl = _tiles(n, t)
    rw_cols = 3 * W + rw_w_up.shape[1] + rw_a_up.shape[1] + rw_g_up.shape[1]
    cdim = ssm_conv_w.shape[1]
    ssm_cols = W + cdim + ssm_a_log.shape[1]
    c1 = rw_cols
    c2 = c1 + ssm_cols
    c3 = c2 + 4 * W
    ssm_pad = (-ssm_cols) % LANES

    x2 = x.reshape(n, d)
    v_first = None
    for layer in range(depth):
        wl = w_in[layer]
        w_rw = wl[:, :c1].astype(BF16)
        w_ssm = jnp.pad(wl[:, c1:c2], ((0, 0), (0, ssm_pad))).astype(BF16)
        w_ret = jnp.concatenate([_deinterleave_heads(wl[:, c2:c2 + W]),
                                 _deinterleave_heads(wl[:, c2 + W:c2 + 2 * W]),
                                 wl[:, c2 + 2 * W:c3]], axis=1).astype(BF16)
        w_gate = wl[:, c3:].astype(BF16)
        p_rw = _norm_proj(x2, norm1_w[layer], w_rw, tl['tm']).reshape(bsz, t, -1)
        p_ssm = _norm_proj(x2, norm1_w[layer], w_ssm, tl['tm']).reshape(bsz, t, -1)
        p_ret = _norm_proj(x2, norm1_w[layer], w_ret, tl['tm']).reshape(bsz, t, -1)
        gate_logits = _norm_proj(x2, norm1_w[layer], w_gate, tl['tm'])

        rw_prm = dict(mu=rw_mu[layer], w_up=rw_w_up[layer], w0=rw_w0[layer], a_up=rw_a_up[layer],
                      a0=rw_a0[layer], g_up=rw_g_up[layer], k_k=rw_k_k[layer], k_a=rw_k_a[layer],
                      r_k=rw_r_k[layer].reshape(-1), gn_w=rw_gn_w[layer], gn_b=rw_gn_b[layer])
        vres = None if layer == 0 else (rw_v_down[layer - 1], rw_v_up[layer - 1], rw_v0[layer - 1])
        y_rw, v_first = _rwkv_mix(p_rw, rw_prm, v_first, vres, tl['tt'])
        ssm_prm = dict(conv_w=ssm_conv_w[layer], conv_b=ssm_conv_b[layer], dt_bias=ssm_dt_bias[layer],
                       a_log=ssm_a_log[layer], d=ssm_d[layer], norm_w=ssm_norm_w[layer])
        y_ssm = _ssm_mix(p_ssm, ssm_prm)
        y_ret = _ret_mix(p_ret)
        ys = [y.reshape(n, W) for y in (y_rw, y_ssm, y_ret)]
        x2 = _merge(x2, gate_logits, ys, w_branch[layer], w_out[layer], tl['tm'])

        j = layer // 2
        final_w = final_norm_w if layer == depth - 1 else None
        if layer % 2 == 0:
            x2 = _ffn(x2, norm2_w[layer], ffn_wg[j].astype(BF16), ffn_wu[j].astype(BF16),
                      ffn_wd[j].astype(BF16), final_w, tl['tm'], _ff_tile(ffn_wg.shape[2]))
        else:
            x2 = _moe(x2, norm2_w[layer], moe_wg[j].astype(BF16), moe_wu[j].astype(BF16),
                      moe_wd[j].astype(BF16), moe_router[j], final_w, tl['tm_moe'],
                      _ff_tile(moe_wg.shape[3], 1024), tl['cap'])
    return x2.reshape(bsz, t, d)
```

```python
import functools
import math

import jax
import jax.numpy as jnp
from jax import lax
from jax.experimental import pallas as pl
from jax.experimental.pallas import tpu as pltpu

F32 = jnp.float32
BF16 = jnp.bfloat16

LANES = 128
SUBLANES = 8
VMEM_LIMIT_BYTES = 56 * 1024 * 1024

RMS_EPS = 1e-6
RW_GN_EPS = 64e-5
HEADS = 8
HEAD_DIM = 64
WIDTH = HEADS * HEAD_DIM
RW_LORA_PAD = 128
RW_MV_LORA = 32
SSM_GROUPS = 2
SSM_STATE = 128
SSM_CONV = 4
ROPE_BASE = 10000.0
N_EXPERTS = 8

RW_CHUNK = 64
SSM_CHUNK = 128
RET_CHUNK = 128


def _cparams(*sem):
    return pltpu.CompilerParams(dimension_semantics=sem, vmem_limit_bytes=VMEM_LIMIT_BYTES)


def _dot(a, b):
    return jnp.dot(a.astype(BF16), b.astype(BF16), preferred_element_type=F32)


def _dot_nt(a, b):
    return lax.dot_general(a.astype(BF16), b.astype(BF16), (((1,), (1,)), ((), ())),
                           preferred_element_type=F32)


def _dot_tn(a, b):
    return lax.dot_general(a.astype(BF16), b.astype(BF16), (((0,), (0,)), ((), ())),
                           preferred_element_type=F32)


def _bmm(a, b):
    return lax.dot_general(a.astype(BF16), b.astype(BF16), (((2,), (1,)), ((0,), (0,))),
                           preferred_element_type=F32)


def _bmm_nt(a, b):
    return lax.dot_general(a.astype(BF16), b.astype(BF16), (((2,), (2,)), ((0,), (0,))),
                           preferred_element_type=F32)


def _bmm_tn(a, b):
    return lax.dot_general(a.astype(BF16), b.astype(BF16), (((1,), (1,)), ((0,), (0,))),
                           preferred_element_type=F32)


def _split2(x):
    hi = x.astype(BF16)
    lo = (x - hi.astype(F32)).astype(BF16)
    return hi, lo


def _split3(x):
    hi = x.astype(BF16)
    r = x - hi.astype(F32)
    mid = r.astype(BF16)
    lo = (r - mid.astype(F32)).astype(BF16)
    return hi, mid, lo


def _dot_exact_rhs(x, m_bf16):
    hi, mid, lo = _split3(x)
    d = lambda a: jnp.dot(a, m_bf16, preferred_element_type=F32)
    return d(hi) + d(mid) + d(lo)


def _dot_exact_lhs(m_bf16, x):
    hi, mid, lo = _split3(x)
    d = lambda a: jnp.dot(m_bf16, a, preferred_element_type=F32)
    return d(hi) + d(mid) + d(lo)


def _sigmoid(x):
    return 1.0 / (1.0 + jnp.exp(-x))


def _silu(x):
    return x * _sigmoid(x)


def _softplus(x):
    return jnp.maximum(x, 0.0) + jnp.log(1.0 + jnp.exp(-jnp.abs(x)))


def _rms_rows(x, w):
    return x * lax.rsqrt(jnp.mean(x * x, axis=-1, keepdims=True) + RMS_EPS) * w


def _norm_proj_kernel(x_ref, nw_ref, w_ref, o_ref):
    h = _rms_rows(x_ref[...], nw_ref[...]).astype(BF16)
    o_ref[...] = jnp.dot(h, w_ref[...], preferred_element_type=F32)


def _norm_proj(x2, norm_w, w_bf16, tm):
    n, d = x2.shape
    c = w_bf16.shape[1]
    return pl.pallas_call(
        _norm_proj_kernel,
        grid=(n // tm,),
        in_specs=[pl.BlockSpec((tm, d), lambda i: (i, 0)),
                  pl.BlockSpec((1, d), lambda i: (0, 0)),
                  pl.BlockSpec((d, c), lambda i: (0, 0))],
        out_specs=pl.BlockSpec((tm, c), lambda i: (i, 0)),
        out_shape=jax.ShapeDtypeStruct((n, c), F32),
        compiler_params=_cparams("parallel"),
        name="norm_proj",
    )(x2, norm_w.reshape(1, d), w_bf16)


def _head_sum(x, blk_ref):
    return _dot_exact_rhs(x, blk_ref[...])


def _rwkv_a_kernel(*refs, tt, has_vres):
    if has_vres:
        (p_ref, pprev_ref, mu_ref, wup_ref, w0_ref, aup_ref, a0_ref, gup_ref, kk_ref, ka_ref,
         rk_ref, blk_ref, vfirst_ref, vdown_ref, vup_ref, v0_ref,
         rq_ref, y0_ref, m_ref, c_ref, bonus_ref, g_ref) = refs
        vout_ref = None
    else:
        (p_ref, pprev_ref, mu_ref, wup_ref, w0_ref, aup_ref, a0_ref, gup_ref, kk_ref, ka_ref,
         rk_ref, blk_ref,
         rq_ref, y0_ref, m_ref, c_ref, bonus_ref, g_ref, vout_ref) = refs
    L = RW_CHUNK
    nchunk = tt // L
    W = WIDTH

    p = p_ref[0]
    prev = jnp.where(pl.program_id(1) == 0, 0.0, pprev_ref[0][SUBLANES - 1:SUBLANES, :])
    row = lax.broadcasted_iota(jnp.int32, (tt, 1), 0)
    shifted = jnp.where(row == 0, prev, pltpu.roll(p, 1, axis=0))
    xm = p + (shifted - p) * mu_ref[...]

    r = xm[:, 0:W]
    k = xm[:, W:2 * W]
    v = xm[:, 2 * W:3 * W]
    xwa = xm[:, 3 * W:3 * W + RW_LORA_PAD]
    xg = xm[:, 3 * W + RW_LORA_PAD:]

    ld = -math.exp(-0.5) * _sigmoid(w0_ref[...] + _dot(jnp.tanh(xwa), wup_ref[...]))
    a_sig = _sigmoid(a0_ref[...] + _dot(xwa, aup_ref[...]))
    g_ref[0] = _dot(_sigmoid(xg), gup_ref[...])
    if has_vres:
        vu = _dot(_dot(v, vdown_ref[...]), vup_ref[...])
        v = v + (vfirst_ref[0] - v) * _sigmoid(v0_ref[...] + vu)
    else:
        vout_ref[0] = v

    kk = k * kk_ref[...]
    kk = kk * lax.rsqrt(jnp.maximum(_head_sum(kk * kk, blk_ref), 1e-24))
    k = k * (1.0 + (a_sig - 1.0) * ka_ref[...])
    bonus_ref[0] = _head_sum(r * k * rk_ref[...], blk_ref) * v
    avec = -kk
    bvec = kk * a_sig

    cum = ld
    pos = row & (L - 1)
    d = 1
    while d < L:
        cum = cum + jnp.where(pos >= d, pltpu.roll(cum, d, axis=0), 0.0)
        d *= 2
    cum3 = cum.reshape(nchunk, L, W)
    last = jnp.broadcast_to(cum3[:, L - 1:L, :], (nchunk, L, W)).reshape(tt, W)
    w_inv = jnp.exp(-cum)
    w_last = jnp.exp(last - cum)

    def to_g(val):
        val = val.astype(BF16)
        return jnp.concatenate(
            [val[:, h * HEAD_DIM:(h + 1) * HEAD_DIM].reshape(nchunk, L, HEAD_DIM) for h in range(HEADS)],
            axis=0)

    at = to_g(avec * jnp.exp(cum - ld))
    rt = to_g(r * jnp.exp(cum))
    bt = to_g(bvec * w_inv)
    kt = to_g(k * w_inv)
    bh = to_g(bvec * w_last)
    kh = to_g(k * w_last)
    vv = to_g(v)
    wl = jnp.concatenate(
        [jnp.exp(cum3[:, L - 1:L, h * HEAD_DIM:(h + 1) * HEAD_DIM]) for h in range(HEADS)], axis=0)

    ri = lax.broadcasted_iota(jnp.int32, (1, L, L), 1)
    ci = lax.broadcasted_iota(jnp.int32, (1, L, L), 2)
    strict = ri > ci
    incl = ri >= ci
    eye = ri == ci

    g4 = _bmm_nt(jnp.concatenate([at, rt], axis=1), jnp.concatenate([bt, kt], axis=1))
    nmat = jnp.where(strict, g4[:, 0:L, 0:L], 0.0)
    a_ak = jnp.where(strict, g4[:, 0:L, L:2 * L], 0.0)
    a_rb = jnp.where(incl, g4[:, L:2 * L, 0:L], 0.0)
    a_rk = jnp.where(incl, g4[:, L:2 * L, L:2 * L], 0.0)
    tp = nmat
    npow = nmat
    span = 1
    while span * 2 < L:
        npow = _bmm(npow, npow)
        tp = tp + npow + _bmm(tp, npow)
        span *= 2
    av = _bmm(jnp.concatenate([a_ak, a_rk], axis=1), vv)
    x0 = jnp.concatenate([at.astype(F32), av[:, 0:L]], axis=2)
    x = x0 + _bmm(tp, x0)
    ry = jnp.concatenate([rt.astype(F32), av[:, L:2 * L]], axis=2) + _bmm(a_rb, x)
    bx = _bmm_tn(bh, x)
    kv = _bmm_tn(kh, vv)
    mc = bx + jnp.concatenate([jnp.where(eye, wl, 0.0), kv], axis=2)

    def pairs(z, out_lo, out_hi):
        z = z.reshape(HEADS // 2, 2, tt, 2 * HEAD_DIM)
        za, zb = z[:, 0], z[:, 1]
        low = lax.broadcasted_iota(jnp.int32, (1, 1, 2 * HEAD_DIM), 2) < HEAD_DIM
        out_lo[0] = jnp.where(low, za, pltpu.roll(zb, HEAD_DIM, axis=2))
        out_hi[0] = jnp.where(low, pltpu.roll(za, HEAD_DIM, axis=2), zb)

    pairs(ry, rq_ref, y0_ref)
    pairs(mc, m_ref, c_ref)


def _rwkv_b_kernel(rq_ref, y0_ref, m_ref, c_ref, bonus_ref, g_ref, gnw_ref, gnb_ref, blk_ref, o_ref,
                   st_s, y_s, *, tt):
    L = RW_CHUNK
    nchunk = tt // L
    bsz = rq_ref.shape[0]
    npair = HEADS // 2
    P = 2 * HEAD_DIM

    @pl.when(pl.program_id(0) == 0)
    def _():
        st_s[...] = jnp.zeros_like(st_s)

    ri = lax.broadcasted_iota(jnp.int32, (P, P), 0) < HEAD_DIM
    ci = lax.broadcasted_iota(jnp.int32, (P, P), 1) < HEAD_DIM
    diag_blocks = ri == ci

    def block_diag(z):
        return jnp.where(diag_blocks, jnp.concatenate([z, z], axis=0), 0.0)

    def chunk_body(c, carry):
        r0 = pl.multiple_of(c * L, L)
        d = lambda a, b_: jnp.dot(a, b_, preferred_element_type=F32)
        for b in range(bsz):
            for p in range(npair):
                st = st_s[b * npair + p]
                rows = (b, p, pl.ds(r0, L), slice(None))
                y = y0_ref[rows] + _dot(rq_ref[rows], st)
                m_hi, m_lo = _split2(block_diag(m_ref[rows]))
                s_hi, s_lo = _split2(st)
                st_s[b * npair + p] = block_diag(c_ref[rows]) + d(m_hi, s_hi) + d(m_lo, s_hi) + d(m_hi, s_lo)
                y_s[b, pl.ds(r0, L), p * P:(p + 1) * P] = y
        return carry

    lax.fori_loop(0, nchunk, chunk_body, 0)

    inv = 1.0 / HEAD_DIM
    for b in range(bsz):
        y = y_s[b]
        mean = _head_sum(y, blk_ref) * inv
        dlt = y - mean
        var = _head_sum(dlt * dlt, blk_ref) * inv
        yn = dlt * lax.rsqrt(var + RW_GN_EPS) * gnw_ref[...] + gnb_ref[...]
        o_ref[b] = ((yn + bonus_ref[b]) * g_ref[b]).astype(o_ref.dtype)


def _rwkv_mix(p_rw, prm, v_first, vres, tt):
    bsz, t, cols = p_rw.shape
    W = WIDTH
    has_vres = vres is not None
    row = lambda a: a.reshape(1, -1).astype(F32)
    full = lambda shape: pl.BlockSpec(shape, lambda b, i: (0,) * len(shape))
    tile = lambda c: pl.BlockSpec((1, tt, c), lambda b, i: (b, i, 0))
    blk = (jnp.arange(W)[:, None] // HEAD_DIM == jnp.arange(W)[None, :] // HEAD_DIM).astype(BF16)

    zpad = jnp.zeros((RW_LORA_PAD // 2, W), F32)
    wup = jnp.concatenate([prm['w_up'], zpad], axis=0).astype(BF16)
    aup = jnp.concatenate([zpad, prm['a_up']], axis=0).astype(BF16)
    ins = [p_rw, p_rw, row(prm['mu']), wup, row(prm['w0']), aup, row(prm['a0']),
           prm['g_up'].astype(BF16), row(prm['k_k']), row(prm['k_a']), row(prm['r_k']), blk]
    in_specs = [tile(cols),
                pl.BlockSpec((1, SUBLANES, cols),
                             lambda b, i: (b, jnp.maximum(i * (tt // SUBLANES) - 1, 0), 0)),
                full((1, cols)), full((RW_LORA_PAD, W)), full((1, W)), full((RW_LORA_PAD, W)),
                full((1, W)), full((RW_LORA_PAD, W)), full((1, W)), full((1, W)), full((1, W)),
                full((W, W))]
    if has_vres:
        v_down, v_up, v0 = vres
        pad = LANES - RW_MV_LORA
        ins += [v_first, jnp.pad(v_down, ((0, 0), (0, pad))).astype(BF16),
                jnp.pad(v_up, ((0, pad), (0, 0))).astype(BF16), row(v0)]
        in_specs += [tile(W), full((W, LANES)), full((LANES, W)), full((1, W))]

    hm = jax.ShapeDtypeStruct((bsz, HEADS // 2, t, 2 * HEAD_DIM), F32)
    tw = jax.ShapeDtypeStruct((bsz, t, W), F32)
    hm_spec = pl.BlockSpec((1, HEADS // 2, tt, 2 * HEAD_DIM), lambda b, i: (b, 0, i, 0))
    out_shape = [hm, hm, hm, hm, tw, tw]
    out_specs = [hm_spec, hm_spec, hm_spec, hm_spec, tile(W), tile(W)]
    if not has_vres:
        out_shape.append(tw)
        out_specs.append(tile(W))
    outs = pl.pallas_call(
        functools.partial(_rwkv_a_kernel, tt=tt, has_vres=has_vres),
        grid=(bsz, t // tt),
        in_specs=in_specs, out_specs=out_specs, out_shape=out_shape,
        compiler_params=_cparams("parallel", "parallel"),
        name="rwkv_a",
    )(*ins)
    if has_vres:
        rq, y0, m, c, bonus, g = outs
    else:
        rq, y0, m, c, bonus, g, v_first = outs

    full1 = lambda shape: pl.BlockSpec(shape, lambda i: (0,) * len(shape))
    tile1 = pl.BlockSpec((bsz, tt, W), lambda i: (0, i, 0))
    hm1 = pl.BlockSpec((bsz, HEADS // 2, tt, 2 * HEAD_DIM), lambda i: (0, 0, i, 0))
    y = pl.pallas_call(
        functools.partial(_rwkv_b_kernel, tt=tt),
        grid=(t // tt,),
        in_specs=[hm1, hm1, hm1, hm1, tile1, tile1, full1((1, W)), full1((1, W)), full1((W, W))],
        out_specs=tile1,
        out_shape=jax.ShapeDtypeStruct((bsz, t, W), BF16),
        scratch_shapes=[pltpu.VMEM((bsz * HEADS // 2, 2 * HEAD_DIM, 2 * HEAD_DIM), F32),
                        pltpu.VMEM((bsz, tt, W), F32)],
        compiler_params=_cparams("arbitrary"),
        name="rwkv_b",
    )(rq, y0, m, c, bonus, g, row(prm['gn_w']), row(prm['gn_b']), blk)
    return y, v_first


def _ssm_kernel(p_ref, pprev_ref, cw_ref, cb_ref, dtb_ref, alog_ref, dsk_ref, nw_ref, exp_ref,
                tri_ref, o_ref, st_s):
    for b in range(p_ref.shape[0]):
        _ssm_one(b, p_ref, pprev_ref, cw_ref, cb_ref, dtb_ref, alog_ref, dsk_ref, nw_ref, exp_ref, tri_ref,
                 o_ref, st_s)


def _ssm_one(b, p_ref, pprev_ref, cw_ref, cb_ref, dtb_ref, alog_ref, dsk_ref, nw_ref, exp_ref, tri_ref,
             o_ref, st_s):
    Q = SSM_CHUNK
    W = WIDTH
    nb = SSM_GROUPS * SSM_STATE
    gw = W // SSM_GROUPS

    @pl.when(pl.program_id(0) == 0)
    def _():
        st_s[b] = jnp.zeros(st_s.shape[1:], F32)

    p = p_ref[b]
    z = p[:, 0:W]
    xbc_in = p[:, W:2 * W + 2 * nb]
    dt_in = p[:, 2 * W + 2 * nb:]
    prev = jnp.where(pl.program_id(0) == 0, 0.0, pprev_ref[b][:, W:2 * W + 2 * nb])
    row = lax.broadcasted_iota(jnp.int32, (Q, 1), 0)
    acc = cb_ref[...] + cw_ref[SSM_CONV - 1:SSM_CONV, :] * xbc_in
    for j in range(1, SSM_CONV):
        sh = pltpu.roll(xbc_in, j, axis=0)
        for i in range(j):
            sh = jnp.where(row == i, prev[SUBLANES - j + i:SUBLANES - j + i + 1, :], sh)
        acc = acc + cw_ref[SSM_CONV - 1 - j:SSM_CONV - j, :] * sh
    xbc = _silu(acc)
    xs = xbc[:, 0:W]
    bm = xbc[:, W:W + nb]
    cm = xbc[:, W + nb:]

    dt = _softplus(dt_in + dtb_ref[...])
    a = -jnp.exp(alog_ref[...]) * dt
    a_cs = _dot_exact_lhs(tri_ref[...], a)
    a_cs_t = a_cs.T
    dt_e = _dot_exact_rhs(dt, exp_ref[...])
    acs_e = _dot_exact_rhs(a_cs, exp_ref[...])
    last_e = acs_e[Q - 1:Q, :]
    xdt = xs * dt_e
    xdec = xdt * jnp.exp(last_e - acs_e)

    li = lax.broadcasted_iota(jnp.int32, (Q, Q), 0)
    si = lax.broadcasted_iota(jnp.int32, (Q, Q), 1)
    causal = li >= si
    hg = HEADS // SSM_GROUPS
    ys = []
    for g in range(SSM_GROUPS):
        bg = bm[:, g * SSM_STATE:(g + 1) * SSM_STATE]
        cg = cm[:, g * SSM_STATE:(g + 1) * SSM_STATE]
        cb = _dot_nt(cg, bg)
        for hh in range(hg):
            h = g * hg + hh
            seg = a_cs[:, h:h + 1] - a_cs_t[h:h + 1, :]
            lmat = jnp.where(causal, jnp.exp(jnp.where(causal, seg, 0.0)), 0.0)
            ys.append(_dot(cb * lmat, xdt[:, h * HEAD_DIM:(h + 1) * HEAD_DIM]))
        st = st_s[b, :, g * gw:(g + 1) * gw]
        y_off = _dot(cg, st) * jnp.exp(acs_e[:, g * gw:(g + 1) * gw])
        ys.append(y_off)
        st_s[b, :, g * gw:(g + 1) * gw] = (jnp.exp(last_e[:, g * gw:(g + 1) * gw]) * st
                                           + _dot_tn(bg, xdec[:, g * gw:(g + 1) * gw]))
    n5 = hg + 1
    y = jnp.concatenate(
        [jnp.concatenate(ys[g * n5:g * n5 + hg], axis=1) + ys[g * n5 + hg] for g in range(SSM_GROUPS)],
        axis=1)
    y = (y + dsk_ref[...] * xs) * _silu(z)
    outs = []
    for g in range(SSM_GROUPS):
        yg = y[:, g * gw:(g + 1) * gw]
        outs.append(yg * lax.rsqrt(jnp.mean(yg * yg, axis=-1, keepdims=True) + RMS_EPS))
    o_ref[b] = (jnp.concatenate(outs, axis=1) * nw_ref[...]).astype(o_ref.dtype)


def _ssm_mix(p_ssm, prm):
    bsz, t, cols = p_ssm.shape
    W = WIDTH
    Q = SSM_CHUNK
    cdim = W + 2 * SSM_GROUPS * SSM_STATE
    row = lambda a: a.reshape(1, -1).astype(F32)
    padrow = lambda a: jnp.pad(a.astype(F32), (0, LANES - a.shape[0])).reshape(1, LANES)
    full = lambda shape: pl.BlockSpec(shape, lambda i: (0,) * len(shape))
    expand = (jnp.arange(LANES)[:, None] == jnp.arange(W)[None, :] // HEAD_DIM).astype(BF16)
    tri = (jnp.arange(Q)[:, None] >= jnp.arange(Q)[None, :]).astype(BF16)
    return pl.pallas_call(
        _ssm_kernel,
        grid=(t // Q,),
        in_specs=[pl.BlockSpec((bsz, Q, cols), lambda i: (0, i, 0)),
                  pl.BlockSpec((bsz, SUBLANES, cols),
                               lambda i: (0, jnp.maximum(i * (Q // SUBLANES) - 1, 0), 0)),
                  full((SSM_CONV, cdim)), full((1, cdim)), full((1, LANES)), full((1, LANES)),
                  full((1, W)), full((1, W)), full((LANES, W)), full((Q, Q))],
        out_specs=pl.BlockSpec((bsz, Q, W), lambda i: (0, i, 0)),
        out_shape=jax.ShapeDtypeStruct((bsz, t, W), BF16),
        scratch_shapes=[pltpu.VMEM((bsz, SSM_STATE, W), F32)],
        compiler_params=_cparams("arbitrary"),
        name="ssm",
    )(p_ssm, p_ssm, prm['conv_w'].T.astype(F32), row(prm['conv_b']), padrow(prm['dt_bias']),
      padrow(prm['a_log']), row(jnp.repeat(prm['d'], HEAD_DIM)), row(prm['norm_w']), expand, tri)


def _ret_log_gamma(h):
    return math.log1p(-(2.0 ** (-5.0 - h)))


def _ret_kernel(p_ref, freq_ref, sgn_ref, blk_ref, o_ref, st_s, dec_s, y_s):
    Q = RET_CHUNK
    W = WIDTH
    half = HEAD_DIM // 2

    @pl.when(pl.program_id(0) == 0)
    def _():
        st_s[...] = jnp.zeros_like(st_s)
        li = lax.broadcasted_iota(jnp.int32, (Q, Q), 0)
        si = lax.broadcasted_iota(jnp.int32, (Q, Q), 1)
        rel = (li - si).astype(F32)
        for h in range(HEADS):
            dec_s[h] = jnp.where(li >= si, jnp.exp(jnp.where(li >= si, rel, 0.0) * _ret_log_gamma(h)), 0.0)

    idx = lax.broadcasted_iota(jnp.int32, (Q, 1), 0)
    pos = (pl.program_id(0) * Q + idx).astype(F32)
    ang = pos * freq_ref[...]
    cos = jnp.concatenate([jnp.cos(ang)] * (W // LANES), axis=1)
    sin = jnp.concatenate([jnp.sin(ang) * sgn_ref[...]] * (W // LANES), axis=1)
    lane = lax.broadcasted_iota(jnp.int32, (1, W), 1)
    first_half = (lane & (HEAD_DIM - 1)) < half

    def rot(x):
        partner = jnp.where(first_half, pltpu.roll(x, W - half, axis=1), pltpu.roll(x, half, axis=1))
        return x * cos + partner * sin

    idf = idx.astype(F32)
    for b in range(p_ref.shape[0]):
        p = p_ref[b]
        q = rot(p[:, 0:W])
        k = rot(p[:, W:2 * W]) * (HEAD_DIM ** -0.5)
        v = p[:, 2 * W:3 * W]
        g = p[:, 3 * W:]
        for h in range(HEADS):
            lg = _ret_log_gamma(h)
            sl = slice(h * HEAD_DIM, (h + 1) * HEAD_DIM)
            qh, kh, vh = q[:, sl], k[:, sl], v[:, sl]
            scores = _dot_nt(qh, kh) * dec_s[h]
            st = st_s[b * HEADS + h]
            y = _dot(scores, vh) + _dot(qh, st) * jnp.exp((idf + 1.0) * lg)
            st_s[b * HEADS + h] = math.exp(Q * lg) * st + _dot_tn(kh * jnp.exp((Q - 1.0 - idf) * lg), vh)
            y_s[:, sl] = y
        y = y_s[...]
        ms = _head_sum(y * y, blk_ref) * (1.0 / HEAD_DIM)
        o_ref[b] = (y * lax.rsqrt(ms + RMS_EPS) * _silu(g)).astype(o_ref.dtype)


def _ret_mix(p_ret):
    bsz, t, cols = p_ret.shape
    W = WIDTH
    Q = RET_CHUNK
    half = HEAD_DIM // 2
    full = lambda shape: pl.BlockSpec(shape, lambda i: (0,) * len(shape))
    inv_freq = ROPE_BASE ** (-jnp.arange(half, dtype=F32) / half)
    freq = jnp.tile(inv_freq, LANES // half).reshape(1, LANES)
    sgn = jnp.where((jnp.arange(LANES) % HEAD_DIM) < half, -1.0, 1.0).astype(F32).reshape(1, LANES)
    blk = (jnp.arange(W)[:, None] // HEAD_DIM == jnp.arange(W)[None, :] // HEAD_DIM).astype(BF16)
    return pl.pallas_call(
        _ret_kernel,
        grid=(t // Q,),
        in_specs=[pl.BlockSpec((bsz, Q, cols), lambda i: (0, i, 0)),
                  full((1, LANES)), full((1, LANES)), full((W, W))],
        out_specs=pl.BlockSpec((bsz, Q, W), lambda i: (0, i, 0)),
        out_shape=jax.ShapeDtypeStruct((bsz, t, W), BF16),
        scratch_shapes=[pltpu.VMEM((bsz * HEADS, HEAD_DIM, HEAD_DIM), F32), pltpu.VMEM((HEADS, Q, Q), F32),
                        pltpu.VMEM((Q, W), F32)],
        compiler_params=_cparams("arbitrary"),
        name="retention",
    )(p_ret, freq, sgn, blk)


def _merge_kernel(x_ref, gl_ref, y0_ref, y1_ref, y2_ref, wb_ref, wo_ref, o_ref):
    d = x_ref.shape[1]
    merged = None
    for i, y_ref in enumerate((y0_ref, y1_ref, y2_ref)):
        gate = _sigmoid(gl_ref[:, i * d:(i + 1) * d])
        term = gate * jnp.dot(y_ref[...], wb_ref[i], preferred_element_type=F32)
        merged = term if merged is None else merged + term
    o_ref[...] = x_ref[...] + _dot(merged, wo_ref[...])


def _merge(x2, gate_logits, ys, w_branch, w_out, tm):
    n, d = x2.shape
    W = WIDTH
    rows = lambda c: pl.BlockSpec((tm, c), lambda i: (i, 0))
    return pl.pallas_call(
        _merge_kernel,
        grid=(n // tm,),
        in_specs=[rows(d), rows(3 * d), rows(W), rows(W), rows(W),
                  pl.BlockSpec((3, W, d), lambda i: (0, 0, 0)),
                  pl.BlockSpec((d, d), lambda i: (0, 0))],
        out_specs=rows(d),
        out_shape=jax.ShapeDtypeStruct((n, d), F32),
        compiler_params=_cparams("parallel"),
        name="merge_out",
    )(x2, gate_logits, *ys, w_branch.astype(BF16), w_out.astype(BF16))


def _ffn_kernel(x_ref, nw_ref, wg_ref, wu_ref, wd_ref, fw_ref, o_ref, h_s, acc_s, *, final_norm):
    j = pl.program_id(1)

    @pl.when(j == 0)
    def _():
        h_s[...] = _rms_rows(x_ref[...], nw_ref[...]).astype(BF16)
        acc_s[...] = jnp.zeros_like(acc_s)

    h = h_s[...]
    act = _silu(jnp.dot(h, wg_ref[...], preferred_element_type=F32)) * jnp.dot(
        h, wu_ref[...], preferred_element_type=F32)
    acc_s[...] += _dot(act, wd_ref[...])

    @pl.when(j == pl.num_programs(1) - 1)
    def _():
        y = x_ref[...] + acc_s[...]
        if final_norm:
            y = _rms_rows(y, fw_ref[...])
        o_ref[...] = y


def _ffn(x2, norm_w, wg, wu, wd, final_w, tm, tf):
    n, d = x2.shape
    f = wg.shape[1]
    final_norm = final_w is not None
    fw = (final_w if final_norm else jnp.ones((d,), F32)).reshape(1, d)
    return pl.pallas_call(
        functools.partial(_ffn_kernel, final_norm=final_norm),
        grid=(n // tm, f // tf),
        in_specs=[pl.BlockSpec((tm, d), lambda i, j: (i, 0)), pl.BlockSpec((1, d), lambda i, j: (0, 0)),
                  pl.BlockSpec((d, tf), lambda i, j: (0, j)), pl.BlockSpec((d, tf), lambda i, j: (0, j)),
                  pl.BlockSpec((tf, d), lambda i, j: (j, 0)), pl.BlockSpec((1, d), lambda i, j: (0, 0))],
        out_specs=pl.BlockSpec((tm, d), lambda i, j: (i, 0)),
        out_shape=jax.ShapeDtypeStruct((n, d), F32),
        scratch_shapes=[pltpu.VMEM((tm, d), BF16), pltpu.VMEM((tm, d), F32)],
        compiler_params=_cparams("parallel", "arbitrary"),
        name="ffn",
    )(x2, norm_w.reshape(1, d), wg, wu, wd, fw)


def _router_kernel(x_ref, nw_ref, rt_ref, h_ref, gate_ref, pos_ref, post_ref, cnt_ref, *, n_experts):
    tm = x_ref.shape[0]
    h = _rms_rows(x_ref[...], nw_ref[...])
    h_ref[...] = h.astype(BF16)
    r_hi, r_lo = rt_ref[0], rt_ref[1]
    h_hi, h_mid, h_lo = _split3(h)
    d = lambda a, b: jnp.dot(a, b, preferred_element_type=F32)
    logits = d(h_hi, r_hi) + d(h_mid, r_hi) + d(h_hi, r_lo) + d(h_lo, r_hi) + d(h_mid, r_lo)
    lane = lax.broadcasted_iota(jnp.int32, logits.shape, 1)
    neg = jnp.float32(-jnp.inf)
    logits = jnp.where(lane < n_experts, logits, neg)
    m1 = jnp.max(logits, axis=-1, keepdims=True)
    i1 = jnp.min(jnp.where(logits == m1, lane, LANES), axis=-1, keepdims=True)
    rest = jnp.where(lane == i1, neg, logits)
    m2 = jnp.max(rest, axis=-1, keepdims=True)
    i2 = jnp.min(jnp.where(rest == m2, lane, LANES), axis=-1, keepdims=True)
    e2 = jnp.exp(m2 - m1)
    w1 = 1.0 / (1.0 + e2)
    gate_ref[...] = jnp.where(lane == i1, w1, 0.0) + jnp.where(lane == i2, e2 * w1, 0.0)
    member = (lane == i1) | (lane == i2)
    m = jnp.where(member, 1.0, 0.0)
    row = lax.broadcasted_iota(jnp.int32, (tm, 1), 0)
    c = m
    step = 1
    while step < tm:
        c = c + jnp.where(row >= step, pltpu.roll(c, step, axis=0), 0.0)
        step *= 2
    posm = jnp.where(member, c - m, -1.0)
    pos_ref[...] = posm
    post_ref[0] = posm.T[0:SUBLANES, :]
    cnt_ref[0] = c[tm - 1:tm, :].astype(jnp.int32)


def _moe_gather_kernel(b_ref, h_ref, post_ref, xs_ref, *, n_experts, cap):
    slot = (lax.broadcasted_iota(jnp.int32, (cap, 1), 0) + b_ref[0] * cap).astype(F32)
    h = h_ref[...]
    for e in range(n_experts):
        sel = jnp.where(post_ref[0, e:e + 1, :] == slot, 1.0, 0.0).astype(BF16)
        xs_ref[e, 0] = jnp.dot(sel, h, preferred_element_type=F32).astype(BF16)


def _moe_expert_kernel(x_ref, wg_ref, wu_ref, wd_ref, o_ref, acc_s):
    j = pl.program_id(2)
    x = x_ref[0]
    act = _silu(jnp.dot(x, wg_ref[0], preferred_element_type=F32)) * jnp.dot(
        x, wu_ref[0], preferred_element_type=F32)
    y = _dot(act, wd_ref[0])

    @pl.when(j == 0)
    def _():
        acc_s[...] = y

    @pl.when(j > 0)
    def _():
        acc_s[...] += y

    @pl.when(j == pl.num_programs(2) - 1)
    def _():
        o_ref[0] = acc_s[...].astype(o_ref.dtype)


def _moe_combine_kernel(flag_ref, acc_ref, gate_ref, pos_ref, ys_ref, fw_ref, o_ref, *, n_experts, cap,
                        final_norm):
    slot = (lax.broadcasted_iota(jnp.int32, (1, cap), 1) + flag_ref[0] * cap).astype(F32)
    y = acc_ref[...]
    pos = pos_ref[...]
    gates = gate_ref[...]
    for e in range(n_experts):
        sel_t = jnp.where(pos[:, e:e + 1] == slot, 1.0, 0.0).astype(BF16)
        y = y + gates[:, e:e + 1] * jnp.dot(sel_t, ys_ref[e, 0], preferred_element_type=F32)
    if final_norm:
        y = jnp.where(flag_ref[1] == 1, _rms_rows(y, fw_ref[...]), y)
    o_ref[...] = y


def _moe(x2, norm_w, wg, wu, wd, router, final_w, tm, tf, cap, rm):
    n, d = x2.shape
    n_experts, _, f = wg.shape
    nt = n // tm
    final_norm = final_w is not None
    fw = (final_w if final_norm else jnp.ones((d,), F32)).reshape(1, d)
    rt = jnp.pad(router.astype(F32), ((0, 0), (0, LANES - n_experts)))
    r_hi = rt.astype(BF16)
    r_lo = (rt - r_hi.astype(F32)).astype(BF16)
    rows = lambda c: pl.BlockSpec((tm, c), lambda i: (i, 0))
    h, gates, pos, pos_t, cnt = pl.pallas_call(
        functools.partial(_router_kernel, n_experts=n_experts),
        grid=(nt,),
        in_specs=[rows(d), pl.BlockSpec((1, d), lambda i: (0, 0)),
                  pl.BlockSpec((2, d, LANES), lambda i: (0, 0, 0))],
        out_specs=[rows(d), rows(LANES), rows(LANES),
                   pl.BlockSpec((1, SUBLANES, tm), lambda i: (i, 0, 0)),
                   pl.BlockSpec((1, 1, LANES), lambda i: (i, 0, 0))],
        out_shape=[jax.ShapeDtypeStruct((n, d), BF16), jax.ShapeDtypeStruct((n, LANES), F32),
                   jax.ShapeDtypeStruct((n, LANES), F32), jax.ShapeDtypeStruct((nt, SUBLANES, tm), F32),
                   jax.ShapeDtypeStruct((nt, 1, LANES), jnp.int32)],
        compiler_params=_cparams("parallel"),
        name="moe_router",
    )(x2, norm_w.reshape(1, d), jnp.stack([r_hi, r_lo]))
    srow = lambda c: pl.BlockSpec((tm, c), lambda i, s: (i, 0))
    slots = pl.BlockSpec((n_experts, 1, cap, d), lambda i, s: (0, i, 0, 0))

    def gather(blk):
        return pl.pallas_call(
            functools.partial(_moe_gather_kernel, n_experts=n_experts, cap=cap),
            grid_spec=pltpu.PrefetchScalarGridSpec(
                num_scalar_prefetch=1, grid=(nt,),
                in_specs=[srow(d), pl.BlockSpec((1, SUBLANES, tm), lambda i, s: (i, 0, 0))],
                out_specs=slots),
            out_shape=jax.ShapeDtypeStruct((n_experts, nt, cap, d), BF16),
            compiler_params=_cparams("parallel"),
            name="moe_gather",
        )(blk.reshape(1), h, pos_t)

    def experts(xs):
        xs = xs.reshape(n_experts, nt * cap, d)
        ys = pl.pallas_call(
            _moe_expert_kernel,
            grid=(n_experts, nt * cap // rm, f // tf),
            in_specs=[pl.BlockSpec((1, rm, d), lambda e, r, j: (e, r, 0)),
                      pl.BlockSpec((1, d, tf), lambda e, r, j: (e, 0, j)),
                      pl.BlockSpec((1, d, tf), lambda e, r, j: (e, 0, j)),
                      pl.BlockSpec((1, tf, d), lambda e, r, j: (e, j, 0))],
            out_specs=pl.BlockSpec((1, rm, d), lambda e, r, j: (e, r, 0)),
            out_shape=jax.ShapeDtypeStruct((n_experts, nt * cap, d), BF16),
            scratch_shapes=[pltpu.VMEM((rm, d), F32)],
            compiler_params=_cparams("parallel", "parallel", "arbitrary"),
            name="moe_experts",
        )(xs, wg, wu, wd)
        return ys.reshape(n_experts, nt, cap, d)

    def combine(blk, last, acc, ys):
        return pl.pallas_call(
            functools.partial(_moe_combine_kernel, n_experts=n_experts, cap=cap, final_norm=final_norm),
            grid_spec=pltpu.PrefetchScalarGridSpec(
                num_scalar_prefetch=1, grid=(nt,),
                in_specs=[srow(d), srow(LANES), srow(LANES), slots, pl.BlockSpec((1, d), lambda i, s: (0, 0))],
                out_specs=srow(d)),
            out_shape=jax.ShapeDtypeStruct((n, d), F32),
            input_output_aliases={1: 0},
            compiler_params=_cparams("parallel"),
            name="moe_combine",
        )(jnp.stack([blk, last.astype(jnp.int32)]), acc, gates, pos, ys, fw)

    nblk = jnp.maximum((jnp.max(cnt[:, 0, :n_experts]) + cap - 1) // cap, 1)
    zero = jnp.int32(0)
    out = combine(zero, nblk == 1, x2, experts(gather(zero)))

    def more(carry):
        blk, acc = carry
        return blk + 1, combine(blk, blk == nblk - 1, acc, experts(gather(blk)))

    return lax.while_loop(lambda c: c[0] < nblk, more, (jnp.int32(1), out))[1]


def _tiles(n_rows, t):
    tm = 512 if n_rows % 512 == 0 else 256
    tt = 256 if t % 256 == 0 else RW_CHUNK
    tm_moe = 1024 if n_rows % 1024 == 0 else tm
    cap = (tm_moe // 4 + tm_moe // 16 + 15) // 16 * 16
    slots = n_rows // tm_moe * cap
    rm = next(r for r in (1024, 640, 512, 320, 256, 128, cap) if slots % r == 0)
    return dict(tm=tm, tt=tt, tm_moe=tm_moe, cap=cap, rm=rm)


def _ff_tile(f, max_tile=2048):
    for parts in (2, 4, 7, 11, 14, 22, 28):
        if f % parts == 0 and (f // parts) % LANES == 0 and f // parts <= max_tile:
            return f // parts
    return f


def _deinterleave_heads(w):
    d_in = w.shape[0]
    return w.reshape(d_in, HEADS, HEAD_DIM // 2, 2).transpose(0, 1, 3, 2).reshape(d_in, WIDTH)


def kernel(x, norm1_w, w_in, rw_mu, rw_w_up, rw_w0, rw_a_up, rw_a0, rw_g_up, rw_k_k, rw_k_a, rw_r_k, rw_gn_w, rw_gn_b, rw_v_down, rw_v_up, rw_v0, ssm_conv_w, ssm_conv_b, ssm_dt_bias, ssm_a_log, ssm_d, ssm_norm_w, w_branch, w_out, norm2_w, ffn_wg, ffn_wu, ffn_wd, moe_router, moe_wg, moe_wu, moe_wd, final_norm_w):
    bsz, t, d = x.shape
    depth = w_in.shape[0]
    n = bsz * t
    W = WIDTH
    tl = _tiles(n, t)
    rw_cols = 3 * W + rw_w_up.shape[1] + rw_a_up.shape[1] + rw_g_up.shape[1]
    cdim = ssm_conv_w.shape[1]
    ssm_cols = W + cdim + ssm_a_log.shape[1]
    c1 = rw_cols
    c2 = c1 + ssm_cols
    c3 = c2 + 4 * W
    ssm_pad = (-ssm_cols) % LANES

    x2 = x.reshape(n, d)
    v_first = None
    for layer in range(depth):
        wl = w_in[layer]
        w_rw = wl[:, :c1].astype(BF16)
        w_ssm = jnp.pad(wl[:, c1:c2], ((0, 0), (0, ssm_pad))).astype(BF16)
        w_ret = jnp.concatenate([_deinterleave_heads(wl[:, c2:c2 + W]),
                                 _deinterleave_heads(wl[:, c2 + W:c2 + 2 * W]),
                                 wl[:, c2 + 2 * W:c3]], axis=1).astype(BF16)
        w_gate = wl[:, c3:].astype(BF16)
        p_rw = _norm_proj(x2, norm1_w[layer], w_rw, tl['tm']).reshape(bsz, t, -1)
        p_ssm = _norm_proj(x2, norm1_w[layer], w_ssm, tl['tm']).reshape(bsz, t, -1)
        p_ret = _norm_proj(x2, norm1_w[layer], w_ret, tl['tm']).reshape(bsz, t, -1)
        gate_logits = _norm_proj(x2, norm1_w[layer], w_gate, tl['tm'])

        rw_prm = dict(mu=rw_mu[layer], w_up=rw_w_up[layer], w0=rw_w0[layer], a_up=rw_a_up[layer],
                      a0=rw_a0[layer], g_up=rw_g_up[layer], k_k=rw_k_k[layer], k_a=rw_k_a[layer],
                      r_k=rw_r_k[layer].reshape(-1), gn_w=rw_gn_w[layer], gn_b=rw_gn_b[layer])
        vres = None if layer == 0 else (rw_v_down[layer - 1], rw_v_up[layer - 1], rw_v0[layer - 1])
        y_rw, v_first = _rwkv_mix(p_rw, rw_prm, v_first, vres, tl['tt'])
        ssm_prm = dict(conv_w=ssm_conv_w[layer], conv_b=ssm_conv_b[layer], dt_bias=ssm_dt_bias[layer],
                       a_log=ssm_a_log[layer], d=ssm_d[layer], norm_w=ssm_norm_w[layer])
        y_ssm = _ssm_mix(p_ssm, ssm_prm)
        y_ret = _ret_mix(p_ret)
        ys = [y.reshape(n, W) for y in (y_rw, y_ssm, y_ret)]
        x2 = _merge(x2, gate_logits, ys, w_branch[layer], w_out[layer], tl['tm'])

        j = layer // 2
        final_w = final_norm_w if layer == depth - 1 else None
        if layer % 2 == 0:
            x2 = _ffn(x2, norm2_w[layer], ffn_wg[j].astype(BF16), ffn_wu[j].astype(BF16),
                      ffn_wd[j].astype(BF16), final_w, tl['tm'], _ff_tile(ffn_wg.shape[2]))
        else:
            x2 = _moe(x2, norm2_w[layer], moe_wg[j].astype(BF16), moe_wu[j].astype(BF16),
                      moe_wd[j].astype(BF16), moe_router[j], final_w, tl['tm_moe'],
                      _ff_tile(moe_wg.shape[3], 1024), tl['cap'], tl['rm'])
    return x2.reshape(bsz, t, d)
```

```python
import functools
import math

import jax
import jax.numpy as jnp
from jax import lax
from jax.experimental import pallas as pl
from jax.experimental.pallas import tpu as pltpu

F32 = jnp.float32
BF16 = jnp.bfloat16

LANES = 128
SUBLANES = 8
VMEM_LIMIT_BYTES = 56 * 1024 * 1024

RMS_EPS = 1e-6
RW_GN_EPS = 64e-5
HEADS = 8
HEAD_DIM = 64
WIDTH = HEADS * HEAD_DIM
RW_LORA_PAD = 128
RW_MV_LORA = 32
SSM_GROUPS = 2
SSM_STATE = 128
SSM_CONV = 4
ROPE_BASE = 10000.0
N_EXPERTS = 8

RW_CHUNK = 64
SSM_CHUNK = 128
RET_CHUNK = 128


def _cparams(*sem):
    return pltpu.CompilerParams(dimension_semantics=sem, vmem_limit_bytes=VMEM_LIMIT_BYTES)


def _dot(a, b):
    return jnp.dot(a.astype(BF16), b.astype(BF16), preferred_element_type=F32)


def _dot_nt(a, b):
    return lax.dot_general(a.astype(BF16), b.astype(BF16), (((1,), (1,)), ((), ())),
                           preferred_element_type=F32)


def _dot_tn(a, b):
    return lax.dot_general(a.astype(BF16), b.astype(BF16), (((0,), (0,)), ((), ())),
                           preferred_element_type=F32)


def _bmm(a, b):
    return lax.dot_general(a.astype(BF16), b.astype(BF16), (((2,), (1,)), ((0,), (0,))),
                           preferred_element_type=F32)


def _bmm_nt(a, b):
    return lax.dot_general(a.astype(BF16), b.astype(BF16), (((2,), (2,)), ((0,), (0,))),
                           preferred_element_type=F32)


def _bmm_tn(a, b):
    return lax.dot_general(a.astype(BF16), b.astype(BF16), (((1,), (1,)), ((0,), (0,))),
                           preferred_element_type=F32)


def _split2(x):
    hi = x.astype(BF16)
    lo = (x - hi.astype(F32)).astype(BF16)
    return hi, lo


def _split3(x):
    hi = x.astype(BF16)
    r = x - hi.astype(F32)
    mid = r.astype(BF16)
    lo = (r - mid.astype(F32)).astype(BF16)
    return hi, mid, lo


def _dot_exact_rhs(x, m_bf16):
    hi, mid, lo = _split3(x)
    d = lambda a: jnp.dot(a, m_bf16, preferred_element_type=F32)
    return d(hi) + d(mid) + d(lo)


def _dot_exact_lhs(m_bf16, x):
    hi, mid, lo = _split3(x)
    d = lambda a: jnp.dot(m_bf16, a, preferred_element_type=F32)
    return d(hi) + d(mid) + d(lo)


def _sigmoid(x):
    return 1.0 / (1.0 + jnp.exp(-x))


def _silu(x):
    return x * _sigmoid(x)


def _softplus(x):
    return jnp.maximum(x, 0.0) + jnp.log(1.0 + jnp.exp(-jnp.abs(x)))


def _rms_rows(x, w):
    return x * lax.rsqrt(jnp.mean(x * x, axis=-1, keepdims=True) + RMS_EPS) * w


def _norm_proj_kernel(x_ref, nw_ref, w_ref, o_ref):
    h = _rms_rows(x_ref[...], nw_ref[...]).astype(BF16)
    o_ref[...] = jnp.dot(h, w_ref[...], preferred_element_type=F32)


def _norm_proj(x2, norm_w, w_bf16, tm):
    n, d = x2.shape
    c = w_bf16.shape[1]
    return pl.pallas_call(
        _norm_proj_kernel,
        grid=(n // tm,),
        in_specs=[pl.BlockSpec((tm, d), lambda i: (i, 0)),
                  pl.BlockSpec((1, d), lambda i: (0, 0)),
                  pl.BlockSpec((d, c), lambda i: (0, 0))],
        out_specs=pl.BlockSpec((tm, c), lambda i: (i, 0)),
        out_shape=jax.ShapeDtypeStruct((n, c), F32),
        compiler_params=_cparams("parallel"),
        name="norm_proj",
    )(x2, norm_w.reshape(1, d), w_bf16)


def _head_sum(x, blk_ref):
    return _dot_exact_rhs(x, blk_ref[...])


def _rwkv_a_kernel(*refs, tt, has_vres):
    if has_vres:
        (p_ref, pprev_ref, mu_ref, wup_ref, w0_ref, aup_ref, a0_ref, gup_ref, kk_ref, ka_ref,
         rk_ref, blk_ref, vfirst_ref, vdown_ref, vup_ref, v0_ref,
         rq_ref, y0_ref, m_ref, c_ref, bonus_ref, g_ref) = refs
        vout_ref = None
    else:
        (p_ref, pprev_ref, mu_ref, wup_ref, w0_ref, aup_ref, a0_ref, gup_ref, kk_ref, ka_ref,
         rk_ref, blk_ref,
         rq_ref, y0_ref, m_ref, c_ref, bonus_ref, g_ref, vout_ref) = refs
    L = RW_CHUNK
    nchunk = tt // L
    W = WIDTH

    p = p_ref[0]
    prev = jnp.where(pl.program_id(1) == 0, 0.0, pprev_ref[0][SUBLANES - 1:SUBLANES, :])
    row = lax.broadcasted_iota(jnp.int32, (tt, 1), 0)
    shifted = jnp.where(row == 0, prev, pltpu.roll(p, 1, axis=0))
    xm = p + (shifted - p) * mu_ref[...]

    r = xm[:, 0:W]
    k = xm[:, W:2 * W]
    v = xm[:, 2 * W:3 * W]
    xwa = xm[:, 3 * W:3 * W + RW_LORA_PAD]
    xg = xm[:, 3 * W + RW_LORA_PAD:]

    ld = -math.exp(-0.5) * _sigmoid(w0_ref[...] + _dot(jnp.tanh(xwa), wup_ref[...]))
    a_sig = _sigmoid(a0_ref[...] + _dot(xwa, aup_ref[...]))
    g_ref[0] = _dot(_sigmoid(xg), gup_ref[...])
    if has_vres:
        vu = _dot(_dot(v, vdown_ref[...]), vup_ref[...])
        v = v + (vfirst_ref[0] - v) * _sigmoid(v0_ref[...] + vu)
    else:
        vout_ref[0] = v

    kk = k * kk_ref[...]
    kk = kk * lax.rsqrt(jnp.maximum(_head_sum(kk * kk, blk_ref), 1e-24))
    k = k * (1.0 + (a_sig - 1.0) * ka_ref[...])
    bonus_ref[0] = _head_sum(r * k * rk_ref[...], blk_ref) * v
    avec = -kk
    bvec = kk * a_sig

    cum = ld
    pos = row & (L - 1)
    d = 1
    while d < L:
        cum = cum + jnp.where(pos >= d, pltpu.roll(cum, d, axis=0), 0.0)
        d *= 2
    cum3 = cum.reshape(nchunk, L, W)
    last = jnp.broadcast_to(cum3[:, L - 1:L, :], (nchunk, L, W)).reshape(tt, W)
    w_inv = jnp.exp(-cum)
    w_last = jnp.exp(last - cum)

    def to_g(val):
        val = val.astype(BF16)
        return jnp.concatenate(
            [val[:, h * HEAD_DIM:(h + 1) * HEAD_DIM].reshape(nchunk, L, HEAD_DIM) for h in range(HEADS)],
            axis=0)

    at = to_g(avec * jnp.exp(cum - ld))
    rt = to_g(r * jnp.exp(cum))
    bt = to_g(bvec * w_inv)
    kt = to_g(k * w_inv)
    bh = to_g(bvec * w_last)
    kh = to_g(k * w_last)
    vv = to_g(v)
    wl = jnp.concatenate(
        [jnp.exp(cum3[:, L - 1:L, h * HEAD_DIM:(h + 1) * HEAD_DIM]) for h in range(HEADS)], axis=0)

    ri = lax.broadcasted_iota(jnp.int32, (1, L, L), 1)
    ci = lax.broadcasted_iota(jnp.int32, (1, L, L), 2)
    strict = ri > ci
    incl = ri >= ci
    eye = ri == ci

    g4 = _bmm_nt(jnp.concatenate([at, rt], axis=1), jnp.concatenate([bt, kt], axis=1))
    nmat = jnp.where(strict, g4[:, 0:L, 0:L], 0.0)
    a_ak = jnp.where(strict, g4[:, 0:L, L:2 * L], 0.0)
    a_rb = jnp.where(incl, g4[:, L:2 * L, 0:L], 0.0)
    a_rk = jnp.where(incl, g4[:, L:2 * L, L:2 * L], 0.0)
    tp = nmat
    npow = nmat
    span = 1
    while span * 2 < L:
        npow = _bmm(npow, npow)
        tp = tp + npow + _bmm(tp, npow)
        span *= 2
    av = _bmm(jnp.concatenate([a_ak, a_rk], axis=1), vv)
    x0 = jnp.concatenate([at.astype(F32), av[:, 0:L]], axis=2)
    x = x0 + _bmm(tp, x0)
    ry = jnp.concatenate([rt.astype(F32), av[:, L:2 * L]], axis=2) + _bmm(a_rb, x)
    bx = _bmm_tn(bh, x)
    kv = _bmm_tn(kh, vv)
    mc = bx + jnp.concatenate([jnp.where(eye, wl, 0.0), kv], axis=2)

    def pairs(z, out_lo, out_hi):
        z = z.reshape(HEADS // 2, 2, tt, 2 * HEAD_DIM)
        za, zb = z[:, 0], z[:, 1]
        low = lax.broadcasted_iota(jnp.int32, (1, 1, 2 * HEAD_DIM), 2) < HEAD_DIM
        out_lo[0] = jnp.where(low, za, pltpu.roll(zb, HEAD_DIM, axis=2))
        out_hi[0] = jnp.where(low, pltpu.roll(za, HEAD_DIM, axis=2), zb)

    pairs(ry, rq_ref, y0_ref)
    pairs(mc, m_ref, c_ref)


def _rwkv_b_kernel(rq_ref, y0_ref, m_ref, c_ref, bonus_ref, g_ref, gnw_ref, gnb_ref, blk_ref, o_ref,
                   st_s, y_s, *, tt):
    L = RW_CHUNK
    nchunk = tt // L
    bsz = rq_ref.shape[0]
    npair = HEADS // 2
    P = 2 * HEAD_DIM

    @pl.when(pl.program_id(0) == 0)
    def _():
        st_s[...] = jnp.zeros_like(st_s)

    ri = lax.broadcasted_iota(jnp.int32, (P, P), 0) < HEAD_DIM
    ci = lax.broadcasted_iota(jnp.int32, (P, P), 1) < HEAD_DIM
    diag_blocks = ri == ci

    def block_diag(z):
        return jnp.where(diag_blocks, jnp.concatenate([z, z], axis=0), 0.0)

    def chunk_body(c, carry):
        r0 = pl.multiple_of(c * L, L)
        d = lambda a, b_: jnp.dot(a, b_, preferred_element_type=F32)
        for b in range(bsz):
            for p in range(npair):
                st = st_s[b * npair + p]
                rows = (b, p, pl.ds(r0, L), slice(None))
                y = y0_ref[rows] + _dot(rq_ref[rows], st)
                m_hi, m_lo = _split2(block_diag(m_ref[rows]))
                s_hi, s_lo = _split2(st)
                st_s[b * npair + p] = block_diag(c_ref[rows]) + d(m_hi, s_hi) + d(m_lo, s_hi) + d(m_hi, s_lo)
                y_s[b, pl.ds(r0, L), p * P:(p + 1) * P] = y
        return carry

    lax.fori_loop(0, nchunk, chunk_body, 0)

    inv = 1.0 / HEAD_DIM
    for b in range(bsz):
        y = y_s[b]
        mean = _head_sum(y, blk_ref) * inv
        dlt = y - mean
        var = _head_sum(dlt * dlt, blk_ref) * inv
        yn = dlt * lax.rsqrt(var + RW_GN_EPS) * gnw_ref[...] + gnb_ref[...]
        o_ref[b] = ((yn + bonus_ref[b]) * g_ref[b]).astype(o_ref.dtype)


def _rwkv_mix(p_rw, prm, v_first, vres, tt):
    bsz, t, cols = p_rw.shape
    W = WIDTH
    has_vres = vres is not None
    row = lambda a: a.reshape(1, -1).astype(F32)
    full = lambda shape: pl.BlockSpec(shape, lambda b, i: (0,) * len(shape))
    tile = lambda c: pl.BlockSpec((1, tt, c), lambda b, i: (b, i, 0))
    blk = (jnp.arange(W)[:, None] // HEAD_DIM == jnp.arange(W)[None, :] // HEAD_DIM).astype(BF16)

    zpad = jnp.zeros((RW_LORA_PAD // 2, W), F32)
    wup = jnp.concatenate([prm['w_up'], zpad], axis=0).astype(BF16)
    aup = jnp.concatenate([zpad, prm['a_up']], axis=0).astype(BF16)
    ins = [p_rw, p_rw, row(prm['mu']), wup, row(prm['w0']), aup, row(prm['a0']),
           prm['g_up'].astype(BF16), row(prm['k_k']), row(prm['k_a']), row(prm['r_k']), blk]
    in_specs = [tile(cols),
                pl.BlockSpec((1, SUBLANES, cols),
                             lambda b, i: (b, jnp.maximum(i * (tt // SUBLANES) - 1, 0), 0)),
                full((1, cols)), full((RW_LORA_PAD, W)), full((1, W)), full((RW_LORA_PAD, W)),
                full((1, W)), full((RW_LORA_PAD, W)), full((1, W)), full((1, W)), full((1, W)),
                full((W, W))]
    if has_vres:
        v_down, v_up, v0 = vres
        pad = LANES - RW_MV_LORA
        ins += [v_first, jnp.pad(v_down, ((0, 0), (0, pad))).astype(BF16),
                jnp.pad(v_up, ((0, pad), (0, 0))).astype(BF16), row(v0)]
        in_specs += [tile(W), full((W, LANES)), full((LANES, W)), full((1, W))]

    hm = jax.ShapeDtypeStruct((bsz, HEADS // 2, t, 2 * HEAD_DIM), F32)
    tw = jax.ShapeDtypeStruct((bsz, t, W), F32)
    hm_spec = pl.BlockSpec((1, HEADS // 2, tt, 2 * HEAD_DIM), lambda b, i: (b, 0, i, 0))
    out_shape = [hm, hm, hm, hm, tw, tw]
    out_specs = [hm_spec, hm_spec, hm_spec, hm_spec, tile(W), tile(W)]
    if not has_vres:
        out_shape.append(tw)
        out_specs.append(tile(W))
    outs = pl.pallas_call(
        functools.partial(_rwkv_a_kernel, tt=tt, has_vres=has_vres),
        grid=(bsz, t // tt),
        in_specs=in_specs, out_specs=out_specs, out_shape=out_shape,
        compiler_params=_cparams("parallel", "parallel"),
        name="rwkv_a",
    )(*ins)
    if has_vres:
        rq, y0, m, c, bonus, g = outs
    else:
        rq, y0, m, c, bonus, g, v_first = outs

    full1 = lambda shape: pl.BlockSpec(shape, lambda i: (0,) * len(shape))
    tile1 = pl.BlockSpec((bsz, tt, W), lambda i: (0, i, 0))
    hm1 = pl.BlockSpec((bsz, HEADS // 2, tt, 2 * HEAD_DIM), lambda i: (0, 0, i, 0))
    y = pl.pallas_call(
        functools.partial(_rwkv_b_kernel, tt=tt),
        grid=(t // tt,),
        in_specs=[hm1, hm1, hm1, hm1, tile1, tile1, full1((1, W)), full1((1, W)), full1((W, W))],
        out_specs=tile1,
        out_shape=jax.ShapeDtypeStruct((bsz, t, W), BF16),
        scratch_shapes=[pltpu.VMEM((bsz * HEADS // 2, 2 * HEAD_DIM, 2 * HEAD_DIM), F32),
                        pltpu.VMEM((bsz, tt, W), F32)],
        compiler_params=_cparams("arbitrary"),
        name="rwkv_b",
    )(rq, y0, m, c, bonus, g, row(prm['gn_w']), row(prm['gn_b']), blk)
    return y, v_first


def _ssm_kernel(p_ref, pprev_ref, cw_ref, cb_ref, dtb_ref, alog_ref, dsk_ref, nw_ref, exp_ref,
                tri_ref, o_ref, st_s):
    for b in range(p_ref.shape[0]):
        _ssm_one(b, p_ref, pprev_ref, cw_ref, cb_ref, dtb_ref, alog_ref, dsk_ref, nw_ref, exp_ref, tri_ref,
                 o_ref, st_s)


def _ssm_one(b, p_ref, pprev_ref, cw_ref, cb_ref, dtb_ref, alog_ref, dsk_ref, nw_ref, exp_ref, tri_ref,
             o_ref, st_s):
    Q = SSM_CHUNK
    W = WIDTH
    nb = SSM_GROUPS * SSM_STATE
    gw = W // SSM_GROUPS

    @pl.when(pl.program_id(0) == 0)
    def _():
        st_s[b] = jnp.zeros(st_s.shape[1:], F32)

    p = p_ref[b]
    z = p[:, 0:W]
    xbc_in = p[:, W:2 * W + 2 * nb]
    dt_in = p[:, 2 * W + 2 * nb:]
    prev = jnp.where(pl.program_id(0) == 0, 0.0, pprev_ref[b][:, W:2 * W + 2 * nb])
    row = lax.broadcasted_iota(jnp.int32, (Q, 1), 0)
    acc = cb_ref[...] + cw_ref[SSM_CONV - 1:SSM_CONV, :] * xbc_in
    for j in range(1, SSM_CONV):
        sh = pltpu.roll(xbc_in, j, axis=0)
        for i in range(j):
            sh = jnp.where(row == i, prev[SUBLANES - j + i:SUBLANES - j + i + 1, :], sh)
        acc = acc + cw_ref[SSM_CONV - 1 - j:SSM_CONV - j, :] * sh
    xbc = _silu(acc)
    xs = xbc[:, 0:W]
    bm = xbc[:, W:W + nb]
    cm = xbc[:, W + nb:]

    dt = _softplus(dt_in + dtb_ref[...])
    a = -jnp.exp(alog_ref[...]) * dt
    a_cs = _dot_exact_lhs(tri_ref[...], a)
    a_cs_t = a_cs.T
    dt_e = _dot_exact_rhs(dt, exp_ref[...])
    acs_e = _dot_exact_rhs(a_cs, exp_ref[...])
    last_e = acs_e[Q - 1:Q, :]
    xdt = xs * dt_e
    xdec = xdt * jnp.exp(last_e - acs_e)

    li = lax.broadcasted_iota(jnp.int32, (Q, Q), 0)
    si = lax.broadcasted_iota(jnp.int32, (Q, Q), 1)
    causal = li >= si
    hg = HEADS // SSM_GROUPS
    ys = []
    for g in range(SSM_GROUPS):
        bg = bm[:, g * SSM_STATE:(g + 1) * SSM_STATE]
        cg = cm[:, g * SSM_STATE:(g + 1) * SSM_STATE]
        cb = _dot_nt(cg, bg)
        for hh in range(hg):
            h = g * hg + hh
            seg = a_cs[:, h:h + 1] - a_cs_t[h:h + 1, :]
            lmat = jnp.where(causal, jnp.exp(jnp.where(causal, seg, 0.0)), 0.0)
            ys.append(_dot(cb * lmat, xdt[:, h * HEAD_DIM:(h + 1) * HEAD_DIM]))
        st = st_s[b, :, g * gw:(g + 1) * gw]
        y_off = _dot(cg, st) * jnp.exp(acs_e[:, g * gw:(g + 1) * gw])
        ys.append(y_off)
        st_s[b, :, g * gw:(g + 1) * gw] = (jnp.exp(last_e[:, g * gw:(g + 1) * gw]) * st
                                           + _dot_tn(bg, xdec[:, g * gw:(g + 1) * gw]))
    n5 = hg + 1
    y = jnp.concatenate(
        [jnp.concatenate(ys[g * n5:g * n5 + hg], axis=1) + ys[g * n5 + hg] for g in range(SSM_GROUPS)],
        axis=1)
    y = (y + dsk_ref[...] * xs) * _silu(z)
    outs = []
    for g in range(SSM_GROUPS):
        yg = y[:, g * gw:(g + 1) * gw]
        outs.append(yg * lax.rsqrt(jnp.mean(yg * yg, axis=-1, keepdims=True) + RMS_EPS))
    o_ref[b] = (jnp.concatenate(outs, axis=1) * nw_ref[...]).astype(o_ref.dtype)


def _ssm_mix(p_ssm, prm):
    bsz, t, cols = p_ssm.shape
    W = WIDTH
    Q = SSM_CHUNK
    cdim = W + 2 * SSM_GROUPS * SSM_STATE
    row = lambda a: a.reshape(1, -1).astype(F32)
    padrow = lambda a: jnp.pad(a.astype(F32), (0, LANES - a.shape[0])).reshape(1, LANES)
    full = lambda shape: pl.BlockSpec(shape, lambda i: (0,) * len(shape))
    expand = (jnp.arange(LANES)[:, None] == jnp.arange(W)[None, :] // HEAD_DIM).astype(BF16)
    tri = (jnp.arange(Q)[:, None] >= jnp.arange(Q)[None, :]).astype(BF16)
    return pl.pallas_call(
        _ssm_kernel,
        grid=(t // Q,),
        in_specs=[pl.BlockSpec((bsz, Q, cols), lambda i: (0, i, 0)),
                  pl.BlockSpec((bsz, SUBLANES, cols),
                               lambda i: (0, jnp.maximum(i * (Q // SUBLANES) - 1, 0), 0)),
                  full((SSM_CONV, cdim)), full((1, cdim)), full((1, LANES)), full((1, LANES)),
                  full((1, W)), full((1, W)), full((LANES, W)), full((Q, Q))],
        out_specs=pl.BlockSpec((bsz, Q, W), lambda i: (0, i, 0)),
        out_shape=jax.ShapeDtypeStruct((bsz, t, W), BF16),
        scratch_shapes=[pltpu.VMEM((bsz, SSM_STATE, W), F32)],
        compiler_params=_cparams("arbitrary"),
        name="ssm",
    )(p_ssm, p_ssm, prm['conv_w'].T.astype(F32), row(prm['conv_b']), padrow(prm['dt_bias']),
      padrow(prm['a_log']), row(jnp.repeat(prm['d'], HEAD_DIM)), row(prm['norm_w']), expand, tri)


def _ret_log_gamma(h):
    return math.log1p(-(2.0 ** (-5.0 - h)))


def _ret_kernel(p_ref, freq_ref, sgn_ref, blk_ref, o_ref, st_s, dec_s, y_s):
    Q = RET_CHUNK
    W = WIDTH
    half = HEAD_DIM // 2

    @pl.when(pl.program_id(0) == 0)
    def _():
        st_s[...] = jnp.zeros_like(st_s)
        li = lax.broadcasted_iota(jnp.int32, (Q, Q), 0)
        si = lax.broadcasted_iota(jnp.int32, (Q, Q), 1)
        rel = (li - si).astype(F32)
        for h in range(HEADS):
            dec_s[h] = jnp.where(li >= si, jnp.exp(jnp.where(li >= si, rel, 0.0) * _ret_log_gamma(h)), 0.0)

    idx = lax.broadcasted_iota(jnp.int32, (Q, 1), 0)
    pos = (pl.program_id(0) * Q + idx).astype(F32)
    ang = pos * freq_ref[...]
    cos = jnp.concatenate([jnp.cos(ang)] * (W // LANES), axis=1)
    sin = jnp.concatenate([jnp.sin(ang) * sgn_ref[...]] * (W // LANES), axis=1)
    lane = lax.broadcasted_iota(jnp.int32, (1, W), 1)
    first_half = (lane & (HEAD_DIM - 1)) < half

    def rot(x):
        partner = jnp.where(first_half, pltpu.roll(x, W - half, axis=1), pltpu.roll(x, half, axis=1))
        return x * cos + partner * sin

    idf = idx.astype(F32)
    for b in range(p_ref.shape[0]):
        p = p_ref[b]
        q = rot(p[:, 0:W])
        k = rot(p[:, W:2 * W]) * (HEAD_DIM ** -0.5)
        v = p[:, 2 * W:3 * W]
        g = p[:, 3 * W:]
        for h in range(HEADS):
            lg = _ret_log_gamma(h)
            sl = slice(h * HEAD_DIM, (h + 1) * HEAD_DIM)
            qh, kh, vh = q[:, sl], k[:, sl], v[:, sl]
            scores = _dot_nt(qh, kh) * dec_s[h]
            st = st_s[b * HEADS + h]
            y = _dot(scores, vh) + _dot(qh, st) * jnp.exp((idf + 1.0) * lg)
            st_s[b * HEADS + h] = math.exp(Q * lg) * st + _dot_tn(kh * jnp.exp((Q - 1.0 - idf) * lg), vh)
            y_s[:, sl] = y
        y = y_s[...]
        ms = _head_sum(y * y, blk_ref) * (1.0 / HEAD_DIM)
        o_ref[b] = (y * lax.rsqrt(ms + RMS_EPS) * _silu(g)).astype(o_ref.dtype)


def _ret_mix(p_ret):
    bsz, t, cols = p_ret.shape
    W = WIDTH
    Q = RET_CHUNK
    half = HEAD_DIM // 2
    full = lambda shape: pl.BlockSpec(shape, lambda i: (0,) * len(shape))
    inv_freq = ROPE_BASE ** (-jnp.arange(half, dtype=F32) / half)
    freq = jnp.tile(inv_freq, LANES // half).reshape(1, LANES)
    sgn = jnp.where((jnp.arange(LANES) % HEAD_DIM) < half, -1.0, 1.0).astype(F32).reshape(1, LANES)
    blk = (jnp.arange(W)[:, None] // HEAD_DIM == jnp.arange(W)[None, :] // HEAD_DIM).astype(BF16)
    return pl.pallas_call(
        _ret_kernel,
        grid=(t // Q,),
        in_specs=[pl.BlockSpec((bsz, Q, cols), lambda i: (0, i, 0)),
                  full((1, LANES)), full((1, LANES)), full((W, W))],
        out_specs=pl.BlockSpec((bsz, Q, W), lambda i: (0, i, 0)),
        out_shape=jax.ShapeDtypeStruct((bsz, t, W), BF16),
        scratch_shapes=[pltpu.VMEM((bsz * HEADS, HEAD_DIM, HEAD_DIM), F32), pltpu.VMEM((HEADS, Q, Q), F32),
                        pltpu.VMEM((Q, W), F32)],
        compiler_params=_cparams("arbitrary"),
        name="retention",
    )(p_ret, freq, sgn, blk)


def _merge_kernel(x_ref, gl_ref, y0_ref, y1_ref, y2_ref, wb_ref, wo_ref, o_ref):
    d = x_ref.shape[1]
    merged = None
    for i, y_ref in enumerate((y0_ref, y1_ref, y2_ref)):
        gate = _sigmoid(gl_ref[:, i * d:(i + 1) * d])
        term = gate * jnp.dot(y_ref[...], wb_ref[i], preferred_element_type=F32)
        merged = term if merged is None else merged + term
    o_ref[...] = x_ref[...] + _dot(merged, wo_ref[...])


def _merge(x2, gate_logits, ys, w_branch, w_out, tm):
    n, d = x2.shape
    W = WIDTH
    rows = lambda c: pl.BlockSpec((tm, c), lambda i: (i, 0))
    return pl.pallas_call(
        _merge_kernel,
        grid=(n // tm,),
        in_specs=[rows(d), rows(3 * d), rows(W), rows(W), rows(W),
                  pl.BlockSpec((3, W, d), lambda i: (0, 0, 0)),
                  pl.BlockSpec((d, d), lambda i: (0, 0))],
        out_specs=rows(d),
        out_shape=jax.ShapeDtypeStruct((n, d), F32),
        compiler_params=_cparams("parallel"),
        name="merge_out",
    )(x2, gate_logits, *ys, w_branch.astype(BF16), w_out.astype(BF16))


def _ffn_kernel(x_ref, nw_ref, wg_ref, wu_ref, wd_ref, fw_ref, o_ref, h_s, acc_s, *, final_norm):
    j = pl.program_id(1)

    @pl.when(j == 0)
    def _():
        h_s[...] = _rms_rows(x_ref[...], nw_ref[...]).astype(BF16)
        acc_s[...] = jnp.zeros_like(acc_s)

    h = h_s[...]
    act = _silu(jnp.dot(h, wg_ref[...], preferred_element_type=F32)) * jnp.dot(
        h, wu_ref[...], preferred_element_type=F32)
    acc_s[...] += _dot(act, wd_ref[...])

    @pl.when(j == pl.num_programs(1) - 1)
    def _():
        y = x_ref[...] + acc_s[...]
        if final_norm:
            y = _rms_rows(y, fw_ref[...])
        o_ref[...] = y


def _ffn(x2, norm_w, wg, wu, wd, final_w, tm, tf):
    n, d = x2.shape
    f = wg.shape[1]
    final_norm = final_w is not None
    fw = (final_w if final_norm else jnp.ones((d,), F32)).reshape(1, d)
    return pl.pallas_call(
        functools.partial(_ffn_kernel, final_norm=final_norm),
        grid=(n // tm, f // tf),
        in_specs=[pl.BlockSpec((tm, d), lambda i, j: (i, 0)), pl.BlockSpec((1, d), lambda i, j: (0, 0)),
                  pl.BlockSpec((d, tf), lambda i, j: (0, j)), pl.BlockSpec((d, tf), lambda i, j: (0, j)),
                  pl.BlockSpec((tf, d), lambda i, j: (j, 0)), pl.BlockSpec((1, d), lambda i, j: (0, 0))],
        out_specs=pl.BlockSpec((tm, d), lambda i, j: (i, 0)),
        out_shape=jax.ShapeDtypeStruct((n, d), F32),
        scratch_shapes=[pltpu.VMEM((tm, d), BF16), pltpu.VMEM((tm, d), F32)],
        compiler_params=_cparams("parallel", "arbitrary"),
        name="ffn",
    )(x2, norm_w.reshape(1, d), wg, wu, wd, fw)


def _router_kernel(x_ref, nw_ref, rt_ref, h_ref, gate_ref, pos_ref, post_ref, cnt_ref, *, n_experts):
    tm = x_ref.shape[0]
    h = _rms_rows(x_ref[...], nw_ref[...])
    h_ref[...] = h.astype(BF16)
    r_hi, r_lo = rt_ref[0], rt_ref[1]
    h_hi, h_mid, h_lo = _split3(h)
    d = lambda a, b: jnp.dot(a, b, preferred_element_type=F32)
    logits = d(h_hi, r_hi) + d(h_mid, r_hi) + d(h_hi, r_lo) + d(h_lo, r_hi) + d(h_mid, r_lo)
    lane = lax.broadcasted_iota(jnp.int32, logits.shape, 1)
    neg = jnp.float32(-jnp.inf)
    logits = jnp.where(lane < n_experts, logits, neg)
    m1 = jnp.max(logits, axis=-1, keepdims=True)
    i1 = jnp.min(jnp.where(logits == m1, lane, LANES), axis=-1, keepdims=True)
    rest = jnp.where(lane == i1, neg, logits)
    m2 = jnp.max(rest, axis=-1, keepdims=True)
    i2 = jnp.min(jnp.where(rest == m2, lane, LANES), axis=-1, keepdims=True)
    e2 = jnp.exp(m2 - m1)
    w1 = 1.0 / (1.0 + e2)
    gate_ref[...] = jnp.where(lane == i1, w1, 0.0) + jnp.where(lane == i2, e2 * w1, 0.0)
    member = (lane == i1) | (lane == i2)
    m = jnp.where(member, 1.0, 0.0)
    row = lax.broadcasted_iota(jnp.int32, (tm, 1), 0)
    c = m
    step = 1
    while step < tm:
        c = c + jnp.where(row >= step, pltpu.roll(c, step, axis=0), 0.0)
        step *= 2
    posm = jnp.where(member, c - m, -1.0)
    pos_ref[...] = posm
    post_ref[0] = posm.T[0:SUBLANES, :]
    cnt_ref[0] = c[tm - 1:tm, :].astype(jnp.int32)


def _moe_gather_kernel(gi, ge, gr, gs, gf, h_ref, post_ref, xs_init_ref, xs_ref, *, br):
    w = pl.program_id(0)
    flags = gf[w]

    @pl.when((flags & 1) == 1)
    def _():
        rank = post_ref[0, pl.ds(ge[w], 1), :]
        slot = jnp.where(rank >= 0.0, rank + gs[w].astype(F32), -1.0)
        rows_i = lax.broadcasted_iota(jnp.int32, (br, 1), 0).astype(F32)
        sel = jnp.where(slot == rows_i, 1.0, 0.0).astype(BF16)
        rows = jnp.dot(sel, h_ref[...], preferred_element_type=F32).astype(BF16)

        @pl.when((flags & 2) == 2)
        def _():
            xs_ref[...] = rows

        @pl.when((flags & 2) == 0)
        def _():
            xs_ref[...] = (xs_ref[...].astype(F32) + rows.astype(F32)).astype(BF16)


def _moe_expert_kernel(eid, act, x_ref, wg_ref, wu_ref, wd_ref, o_ref, acc_s):
    r = pl.program_id(0)
    j = pl.program_id(1)
    last = j == pl.num_programs(1) - 1

    @pl.when(act[r] == 1)
    def _():
        x = x_ref[...]
        a = _silu(jnp.dot(x, wg_ref[0], preferred_element_type=F32)) * jnp.dot(
            x, wu_ref[0], preferred_element_type=F32)
        y = _dot(a, wd_ref[0])

        @pl.when(j == 0)
        def _():
            acc_s[...] = y

        @pl.when(j > 0)
        def _():
            acc_s[...] += y

        @pl.when(last)
        def _():
            o_ref[...] = acc_s[...].astype(o_ref.dtype)

    @pl.when((act[r] == 0) & last)
    def _():
        o_ref[...] = jnp.zeros_like(o_ref)


def _moe_combine_kernel(ci, ce, cr, cs, cf, x_ref, gate_ref, pos_ref, ys_ref, fw_ref, o_ref, *, br, final_norm):
    w = pl.program_id(0)
    flags = cf[w]
    tm = x_ref.shape[0]

    @pl.when((flags & 1) == 1)
    def _():
        @pl.when((flags & 2) == 2)
        def _():
            o_ref[...] = x_ref[...]

        lane = lax.broadcasted_iota(jnp.int32, (tm, LANES), 1)
        pick = lambda ref: jnp.sum(jnp.where(lane == ce[w], ref[...], 0.0), axis=-1, keepdims=True)
        rank = pick(pos_ref)
        slot = jnp.where(rank >= 0.0, rank + cs[w].astype(F32), -1.0)
        cols_i = lax.broadcasted_iota(jnp.int32, (1, br), 1).astype(F32)
        sel_t = jnp.where(slot == cols_i, 1.0, 0.0).astype(BF16)
        o_ref[...] += pick(gate_ref) * jnp.dot(sel_t, ys_ref[...], preferred_element_type=F32)

        if final_norm:
            @pl.when((flags & 4) == 4)
            def _():
                o_ref[...] = _rms_rows(o_ref[...], fw_ref[...])


def _moe_plan(cnt, tm, br, rm):
    nt, n_e = cnt.shape
    s_max = (-(-nt * tm * 2 // rm) + n_e) * rm
    tot = cnt.sum(0)
    ptot = (tot + rm - 1) // rm * rm
    seg_end = jnp.cumsum(ptot)
    off = (seg_end - ptot)[None, :] + jnp.cumsum(cnt, 0) - cnt
    first_blk = off // br
    nb = jnp.where(cnt > 0, (off + cnt - 1) // br - first_blk + 1, 0)
    w_max = s_max // br + nt * n_e
    w = jnp.arange(w_max, dtype=jnp.int32)

    def work_list(order):
        nb_o = nb.reshape(-1)[order]
        incl = jnp.cumsum(nb_o)
        total = incl[-1]
        w_eff = jnp.minimum(w, total - 1)
        p = jnp.clip(jnp.searchsorted(incl, w_eff, side='right'), 0, nt * n_e - 1)
        pair = order[p]
        i_w, e_w = pair // n_e, pair % n_e
        r_w = first_blk.reshape(-1)[pair] + (w_eff - (incl - nb_o)[p])
        shift = off.reshape(-1)[pair] - r_w * br
        return (i_w, e_w, r_w, shift, w < total)

    ids = jnp.arange(nt * n_e, dtype=jnp.int32)
    gi, ge, gr, gs, gv = work_list(ids.reshape(nt, n_e).T.reshape(-1))
    g_first = gv & jnp.concatenate([jnp.ones((1,), bool), gr[1:] != gr[:-1]])
    g_flags = gv.astype(jnp.int32) + 2 * g_first.astype(jnp.int32)
    ci, ce, cr, cs, cv = work_list(ids)
    c_first = cv & jnp.concatenate([jnp.ones((1,), bool), ci[1:] != ci[:-1]])
    nxt_valid = jnp.concatenate([cv[1:], jnp.zeros((1,), bool)])
    c_last = cv & (jnp.concatenate([ci[1:] != ci[:-1], jnp.ones((1,), bool)]) | ~nxt_valid)
    c_flags = cv.astype(jnp.int32) + 2 * c_first.astype(jnp.int32) + 4 * c_last.astype(jnp.int32)
    r0 = jnp.arange(s_max // rm, dtype=jnp.int32) * rm
    eid = jnp.clip(jnp.searchsorted(seg_end, r0, side='right'), 0, n_e - 1)
    act = (r0 < seg_end[-1]).astype(jnp.int32)
    i32 = lambda *xs: tuple(x.astype(jnp.int32) for x in xs)
    return s_max, i32(gi, ge, gr, gs, g_flags), i32(ci, ce, cr, cs, c_flags), i32(eid, act)


def _moe(x2, norm_w, wg, wu, wd, router, final_w, tm, tf, br, rm):
    n, d = x2.shape
    n_experts, _, f = wg.shape
    nt = n // tm
    final_norm = final_w is not None
    fw = (final_w if final_norm else jnp.ones((d,), F32)).reshape(1, d)
    rt = jnp.pad(router.astype(F32), ((0, 0), (0, LANES - n_experts)))
    r_hi = rt.astype(BF16)
    r_lo = (rt - r_hi.astype(F32)).astype(BF16)
    rows = lambda c: pl.BlockSpec((tm, c), lambda i: (i, 0))
    h, gates, pos, pos_t, cnt = pl.pallas_call(
        functools.partial(_router_kernel, n_experts=n_experts),
        grid=(nt,),
        in_specs=[rows(d), pl.BlockSpec((1, d), lambda i: (0, 0)),
                  pl.BlockSpec((2, d, LANES), lambda i: (0, 0, 0))],
        out_specs=[rows(d), rows(LANES), rows(LANES),
                   pl.BlockSpec((1, SUBLANES, tm), lambda i: (i, 0, 0)),
                   pl.BlockSpec((1, 1, LANES), lambda i: (i, 0, 0))],
        out_shape=[jax.ShapeDtypeStruct((n, d), BF16), jax.ShapeDtypeStruct((n, LANES), F32),
                   jax.ShapeDtypeStruct((n, LANES), F32), jax.ShapeDtypeStruct((nt, SUBLANES, tm), F32),
                   jax.ShapeDtypeStruct((nt, 1, LANES), jnp.int32)],
        compiler_params=_cparams("parallel"),
        name="moe_router",
    )(x2, norm_w.reshape(1, d), jnp.stack([r_hi, r_lo]))
    s_max, g_meta, c_meta, e_meta = _moe_plan(cnt[:, 0, :n_experts], tm, br, rm)
    n_work = g_meta[0].shape[0]

    xs = pl.pallas_call(
        functools.partial(_moe_gather_kernel, br=br),
        grid_spec=pltpu.PrefetchScalarGridSpec(
            num_scalar_prefetch=5, grid=(n_work,),
            in_specs=[pl.BlockSpec((tm, d), lambda w, gi, ge, gr, gs, gf: (gi[w], 0)),
                      pl.BlockSpec((1, SUBLANES, tm), lambda w, gi, ge, gr, gs, gf: (gi[w], 0, 0)),
                      pl.BlockSpec(memory_space=pl.ANY)],
            out_specs=pl.BlockSpec((br, d), lambda w, gi, ge, gr, gs, gf: (gr[w], 0))),
        out_shape=jax.ShapeDtypeStruct((s_max, d), BF16),
        input_output_aliases={7: 0},
        compiler_params=_cparams("arbitrary"),
        name="moe_gather",
    )(*g_meta, h, pos_t, jnp.zeros((s_max, d), BF16))

    nj = f // tf
    jj = lambda j, r, act: j * act[r] + (nj - 1) * (1 - act[r])
    ys = pl.pallas_call(
        _moe_expert_kernel,
        grid_spec=pltpu.PrefetchScalarGridSpec(
            num_scalar_prefetch=2, grid=(s_max // rm, nj),
            in_specs=[pl.BlockSpec((rm, d), lambda r, j, eid, act: (r, 0)),
                      pl.BlockSpec((1, d, tf), lambda r, j, eid, act: (eid[r], 0, jj(j, r, act))),
                      pl.BlockSpec((1, d, tf), lambda r, j, eid, act: (eid[r], 0, jj(j, r, act))),
                      pl.BlockSpec((1, tf, d), lambda r, j, eid, act: (eid[r], jj(j, r, act), 0))],
            out_specs=pl.BlockSpec((rm, d), lambda r, j, eid, act: (r, 0)),
            scratch_shapes=[pltpu.VMEM((rm, d), F32)]),
        out_shape=jax.ShapeDtypeStruct((s_max, d), BF16),
        compiler_params=_cparams("parallel", "arbitrary"),
        name="moe_experts",
    )(*e_meta, xs, wg, wu, wd)

    tile = lambda c: pl.BlockSpec((tm, c), lambda w, ci, ce, cr, cs, cf: (ci[w], 0))
    return pl.pallas_call(
        functools.partial(_moe_combine_kernel, br=br, final_norm=final_norm),
        grid_spec=pltpu.PrefetchScalarGridSpec(
            num_scalar_prefetch=5, grid=(n_work,),
            in_specs=[tile(d), tile(LANES), tile(LANES),
                      pl.BlockSpec((br, d), lambda w, ci, ce, cr, cs, cf: (cr[w], 0)),
                      pl.BlockSpec((1, d), lambda w, ci, ce, cr, cs, cf: (0, 0))],
            out_specs=tile(d)),
        out_shape=jax.ShapeDtypeStruct((n, d), F32),
        compiler_params=_cparams("arbitrary"),
        name="moe_combine",
    )(*c_meta, x2, gates, pos, ys, fw)


def _tiles(n_rows, t):
    tm = 512 if n_rows % 512 == 0 else 256
    tt = 256 if t % 256 == 0 else RW_CHUNK
    tm_moe = 1024 if n_rows % 1024 == 0 else tm
    br, rm = 256, 512
    return dict(tm=tm, tt=tt, tm_moe=tm_moe, br=br, rm=rm)


def _ff_tile(f, max_tile=2048):
    for parts in (2, 4, 7, 11, 14, 22, 28):
        if f % parts == 0 and (f // parts) % LANES == 0 and f // parts <= max_tile:
            return f // parts
    return f


def _deinterleave_heads(w):
    d_in = w.shape[0]
    return w.reshape(d_in, HEADS, HEAD_DIM // 2, 2).transpose(0, 1, 3, 2).reshape(d_in, WIDTH)


def kernel(x, norm1_w, w_in, rw_mu, rw_w_up, rw_w0, rw_a_up, rw_a0, rw_g_up, rw_k_k, rw_k_a, rw_r_k, rw_gn_w, rw_gn_b, rw_v_down, rw_v_up, rw_v0, ssm_conv_w, ssm_conv_b, ssm_dt_bias, ssm_a_log, ssm_d, ssm_norm_w, w_branch, w_out, norm2_w, ffn_wg, ffn_wu, ffn_wd, moe_router, moe_wg, moe_wu, moe_wd, final_norm_w):
    bsz, t, d = x.shape
    depth = w_in.shape[0]
    n = bsz * t
    W = WIDTH
    tl = _tiles(n, t)
    rw_cols = 3 * W + rw_w_up.shape[1] + rw_a_up.shape[1] + rw_g_up.shape[1]
    cdim = ssm_conv_w.shape[1]
    ssm_cols = W + cdim + ssm_a_log.shape[1]
    c1 = rw_cols
    c2 = c1 + ssm_cols
    c3 = c2 + 4 * W
    ssm_pad = (-ssm_cols) % LANES

    x2 = x.reshape(n, d)
    v_first = None
    for layer in range(depth):
        wl = w_in[layer]
        w_rw = wl[:, :c1].astype(BF16)
        w_ssm = jnp.pad(wl[:, c1:c2], ((0, 0), (0, ssm_pad))).astype(BF16)
        w_ret = jnp.concatenate([_deinterleave_heads(wl[:, c2:c2 + W]),
                                 _deinterleave_heads(wl[:, c2 + W:c2 + 2 * W]),
                                 wl[:, c2 + 2 * W:c3]], axis=1).astype(BF16)
        w_gate = wl[:, c3:].astype(BF16)
        p_rw = _norm_proj(x2, norm1_w[layer], w_rw, tl['tm']).reshape(bsz, t, -1)
        p_ssm = _norm_proj(x2, norm1_w[layer], w_ssm, tl['tm']).reshape(bsz, t, -1)
        p_ret = _norm_proj(x2, norm1_w[layer], w_ret, tl['tm']).reshape(bsz, t, -1)
        gate_logits = _norm_proj(x2, norm1_w[layer], w_gate, tl['tm'])

        rw_prm = dict(mu=rw_mu[layer], w_up=rw_w_up[layer], w0=rw_w0[layer], a_up=rw_a_up[layer],
                      a0=rw_a0[layer], g_up=rw_g_up[layer], k_k=rw_k_k[layer], k_a=rw_k_a[layer],
                      r_k=rw_r_k[layer].reshape(-1), gn_w=rw_gn_w[layer], gn_b=rw_gn_b[layer])
        vres = None if layer == 0 else (rw_v_down[layer - 1], rw_v_up[layer - 1], rw_v0[layer - 1])
        y_rw, v_first = _rwkv_mix(p_rw, rw_prm, v_first, vres, tl['tt'])
        ssm_prm = dict(conv_w=ssm_conv_w[layer], conv_b=ssm_conv_b[layer], dt_bias=ssm_dt_bias[layer],
                       a_log=ssm_a_log[layer], d=ssm_d[layer], norm_w=ssm_norm_w[layer])
        y_ssm = _ssm_mix(p_ssm, ssm_prm)
        y_ret = _ret_mix(p_ret)
        ys = [y.reshape(n, W) for y in (y_rw, y_ssm, y_ret)]
        x2 = _merge(x2, gate_logits, ys, w_branch[layer], w_out[layer], tl['tm'])

        j = layer // 2
        final_w = final_norm_w if layer == depth - 1 else None
        if layer % 2 == 0:
            x2 = _ffn(x2, norm2_w[layer], ffn_wg[j].astype(BF16), ffn_wu[j].astype(BF16),
                      ffn_wd[j].astype(BF16), final_w, tl['tm'], _ff_tile(ffn_wg.shape[2]))
        else:
            x2 = _moe(x2, norm2_w[layer], moe_wg[j].astype(BF16), moe_wu[j].astype(BF16),
                      moe_wd[j].astype(BF16), moe_router[j], final_w, tl['tm_moe'],
                      _ff_tile(moe_wg.shape[3], 1024), tl['br'], tl['rm'])
    return x2.reshape(bsz, t, d)
```

```python
import functools
import math

import jax
import jax.numpy as jnp
from jax import lax
from jax.experimental import pallas as pl
from jax.experimental.pallas import tpu as pltpu

F32 = jnp.float32
BF16 = jnp.bfloat16

LANES = 128
SUBLANES = 8
VMEM_LIMIT_BYTES = 56 * 1024 * 1024

RMS_EPS = 1e-6
RW_GN_EPS = 64e-5
HEADS = 8
HEAD_DIM = 64
WIDTH = HEADS * HEAD_DIM
RW_LORA_PAD = 128
RW_MV_LORA = 32
SSM_GROUPS = 2
SSM_STATE = 128
SSM_CONV = 4
ROPE_BASE = 10000.0
N_EXPERTS = 8

RW_CHUNK = 64
SSM_CHUNK = 128
RET_CHUNK = 128


def _cparams(*sem):
    return pltpu.CompilerParams(dimension_semantics=sem, vmem_limit_bytes=VMEM_LIMIT_BYTES)


def _dot(a, b):
    return jnp.dot(a.astype(BF16), b.astype(BF16), preferred_element_type=F32)


def _dot_nt(a, b):
    return lax.dot_general(a.astype(BF16), b.astype(BF16), (((1,), (1,)), ((), ())),
                           preferred_element_type=F32)


def _dot_tn(a, b):
    return lax.dot_general(a.astype(BF16), b.astype(BF16), (((0,), (0,)), ((), ())),
                           preferred_element_type=F32)


def _bmm(a, b):
    return lax.dot_general(a.astype(BF16), b.astype(BF16), (((2,), (1,)), ((0,), (0,))),
                           preferred_element_type=F32)


def _bmm_nt(a, b):
    return lax.dot_general(a.astype(BF16), b.astype(BF16), (((2,), (2,)), ((0,), (0,))),
                           preferred_element_type=F32)


def _bmm_tn(a, b):
    return lax.dot_general(a.astype(BF16), b.astype(BF16), (((1,), (1,)), ((0,), (0,))),
                           preferred_element_type=F32)


def _split2(x):
    hi = x.astype(BF16)
    lo = (x - hi.astype(F32)).astype(BF16)
    return hi, lo


def _split3(x):
    hi = x.astype(BF16)
    r = x - hi.astype(F32)
    mid = r.astype(BF16)
    lo = (r - mid.astype(F32)).astype(BF16)
    return hi, mid, lo


def _dot_exact_rhs(x, m_bf16):
    hi, mid, lo = _split3(x)
    d = lambda a: jnp.dot(a, m_bf16, preferred_element_type=F32)
    return d(hi) + d(mid) + d(lo)


def _dot_exact_lhs(m_bf16, x):
    hi, mid, lo = _split3(x)
    d = lambda a: jnp.dot(m_bf16, a, preferred_element_type=F32)
    return d(hi) + d(mid) + d(lo)


def _sigmoid(x):
    return 1.0 / (1.0 + jnp.exp(-x))


def _silu(x):
    return x * _sigmoid(x)


def _softplus(x):
    return jnp.maximum(x, 0.0) + jnp.log(1.0 + jnp.exp(-jnp.abs(x)))


def _rms_rows(x, w):
    return x * lax.rsqrt(jnp.mean(x * x, axis=-1, keepdims=True) + RMS_EPS) * w


def _norm_proj_kernel(x_ref, nw_ref, w_ref, o_ref):
    h = _rms_rows(x_ref[...], nw_ref[...]).astype(BF16)
    o_ref[...] = jnp.dot(h, w_ref[...], preferred_element_type=F32).astype(o_ref.dtype)


def _norm_proj(x2, norm_w, w_bf16, tm, out_dtype=F32):
    n, d = x2.shape
    c = w_bf16.shape[1]
    return pl.pallas_call(
        _norm_proj_kernel,
        grid=(n // tm,),
        in_specs=[pl.BlockSpec((tm, d), lambda i: (i, 0)),
                  pl.BlockSpec((1, d), lambda i: (0, 0)),
                  pl.BlockSpec((d, c), lambda i: (0, 0))],
        out_specs=pl.BlockSpec((tm, c), lambda i: (i, 0)),
        out_shape=jax.ShapeDtypeStruct((n, c), out_dtype),
        compiler_params=_cparams("parallel"),
        name="norm_proj",
    )(x2, norm_w.reshape(1, d), w_bf16)


def _head_sum(x, blk_ref):
    hi, lo = _split2(x)
    m = blk_ref[...]
    return jnp.dot(hi, m, preferred_element_type=F32) + jnp.dot(lo, m, preferred_element_type=F32)


def _rwkv_a_kernel(*refs, tt, has_vres):
    if has_vres:
        (p_ref, pprev_ref, mu_ref, wup_ref, w0_ref, aup_ref, a0_ref, gup_ref, kk_ref, ka_ref,
         rk_ref, blk_ref, vfirst_ref, vdown_ref, vup_ref, v0_ref,
         rq_ref, y0_ref, m_ref, c_ref, bonus_ref, g_ref) = refs
        vout_ref = None
    else:
        (p_ref, pprev_ref, mu_ref, wup_ref, w0_ref, aup_ref, a0_ref, gup_ref, kk_ref, ka_ref,
         rk_ref, blk_ref,
         rq_ref, y0_ref, m_ref, c_ref, bonus_ref, g_ref, vout_ref) = refs
    L = RW_CHUNK
    nchunk = tt // L
    W = WIDTH

    p = p_ref[0].astype(F32)
    nprev = pprev_ref.shape[1]
    prev = jnp.where(pl.program_id(1) == 0, 0.0, pprev_ref[0][nprev - 1:nprev, :].astype(F32))
    row = lax.broadcasted_iota(jnp.int32, (tt, 1), 0)
    shifted = jnp.where(row == 0, prev, pltpu.roll(p, 1, axis=0))
    xm = p + (shifted - p) * mu_ref[...]

    r = xm[:, 0:W]
    k = xm[:, W:2 * W]
    v = xm[:, 2 * W:3 * W]
    xwa = xm[:, 3 * W:3 * W + RW_LORA_PAD]
    xg = xm[:, 3 * W + RW_LORA_PAD:]

    ld = -math.exp(-0.5) * _sigmoid(w0_ref[...] + _dot(jnp.tanh(xwa), wup_ref[...]))
    a_sig = _sigmoid(a0_ref[...] + _dot(xwa, aup_ref[...]))
    g_ref[0] = _dot(_sigmoid(xg), gup_ref[...])
    if has_vres:
        vu = _dot(_dot(v, vdown_ref[...]), vup_ref[...])
        v = v + (vfirst_ref[0] - v) * _sigmoid(v0_ref[...] + vu)
    else:
        vout_ref[0] = v

    kk = k * kk_ref[...]
    kk = kk * lax.rsqrt(jnp.maximum(_head_sum(kk * kk, blk_ref), 1e-24))
    k = k * (1.0 + (a_sig - 1.0) * ka_ref[...])
    bonus_ref[0] = _head_sum(r * k * rk_ref[...], blk_ref) * v
    avec = -kk
    bvec = kk * a_sig

    cum = ld
    pos = row & (L - 1)
    d = 1
    while d < L:
        cum = cum + jnp.where(pos >= d, pltpu.roll(cum, d, axis=0), 0.0)
        d *= 2
    cum3 = cum.reshape(nchunk, L, W)
    last = jnp.broadcast_to(cum3[:, L - 1:L, :], (nchunk, L, W)).reshape(tt, W)
    w_inv = jnp.exp(-cum)
    w_last = jnp.exp(last - cum)

    def to_g(val):
        val = val.astype(BF16)
        return jnp.concatenate(
            [val[:, h * HEAD_DIM:(h + 1) * HEAD_DIM].reshape(nchunk, L, HEAD_DIM) for h in range(HEADS)],
            axis=0)

    at = to_g(avec * jnp.exp(cum - ld))
    rt = to_g(r * jnp.exp(cum))
    bt = to_g(bvec * w_inv)
    kt = to_g(k * w_inv)
    bh = to_g(bvec * w_last)
    kh = to_g(k * w_last)
    vv = to_g(v)
    wl = jnp.concatenate(
        [jnp.exp(cum3[:, L - 1:L, h * HEAD_DIM:(h + 1) * HEAD_DIM]) for h in range(HEADS)], axis=0)

    ri = lax.broadcasted_iota(jnp.int32, (1, L, L), 1)
    ci = lax.broadcasted_iota(jnp.int32, (1, L, L), 2)
    strict = ri > ci
    incl = ri >= ci
    eye = ri == ci

    g4 = _bmm_nt(jnp.concatenate([at, rt], axis=1), jnp.concatenate([bt, kt], axis=1))
    nmat = jnp.where(strict, g4[:, 0:L, 0:L], 0.0)
    a_ak = jnp.where(strict, g4[:, 0:L, L:2 * L], 0.0)
    a_rb = jnp.where(incl, g4[:, L:2 * L, 0:L], 0.0)
    a_rk = jnp.where(incl, g4[:, L:2 * L, L:2 * L], 0.0)
    tp = nmat
    npow = nmat
    span = 1
    while span * 2 < L:
        npow = _bmm(npow, npow)
        tp = tp + npow + _bmm(tp, npow)
        span *= 2
    av = _bmm(jnp.concatenate([a_ak, a_rk], axis=1), vv)
    x0 = jnp.concatenate([at.astype(F32), av[:, 0:L]], axis=2)
    x = x0 + _bmm(tp, x0)
    ry = jnp.concatenate([rt.astype(F32), av[:, L:2 * L]], axis=2) + _bmm(a_rb, x)
    bx = _bmm_tn(bh, x)
    kv = _bmm_tn(kh, vv)
    mc = bx + jnp.concatenate([jnp.where(eye, wl, 0.0), kv], axis=2)

    def pairs(z, out_lo, out_hi):
        z = z.reshape(HEADS // 2, 2, tt, 2 * HEAD_DIM)
        za, zb = z[:, 0], z[:, 1]
        low = lax.broadcasted_iota(jnp.int32, (1, 1, 2 * HEAD_DIM), 2) < HEAD_DIM
        out_lo[0] = jnp.where(low, za, pltpu.roll(zb, HEAD_DIM, axis=2))
        out_hi[0] = jnp.where(low, pltpu.roll(za, HEAD_DIM, axis=2), zb)

    pairs(ry, rq_ref, y0_ref)
    pairs(mc, m_ref, c_ref)


def _rwkv_b_kernel(rq_ref, y0_ref, m_ref, c_ref, bonus_ref, g_ref, gnw_ref, gnb_ref, blk_ref, o_ref,
                   st_s, y_s, *, tt):
    L = RW_CHUNK
    nchunk = tt // L
    bsz = rq_ref.shape[0]
    npair = HEADS // 2
    P = 2 * HEAD_DIM

    @pl.when(pl.program_id(0) == 0)
    def _():
        st_s[...] = jnp.zeros_like(st_s)

    ri = lax.broadcasted_iota(jnp.int32, (P, P), 0) < HEAD_DIM
    ci = lax.broadcasted_iota(jnp.int32, (P, P), 1) < HEAD_DIM
    diag_blocks = ri == ci

    def block_diag(z):
        return jnp.where(diag_blocks, jnp.concatenate([z, z], axis=0), 0.0)

    def chunk_body(c, carry):
        r0 = pl.multiple_of(c * L, L)
        d = lambda a, b_: jnp.dot(a, b_, preferred_element_type=F32)
        for b in range(bsz):
            for p in range(npair):
                st = st_s[b * npair + p]
                rows = (b, p, pl.ds(r0, L), slice(None))
                y = y0_ref[rows] + _dot(rq_ref[rows], st)
                m_hi, m_lo = _split2(block_diag(m_ref[rows]))
                s_hi, s_lo = _split2(st)
                st_s[b * npair + p] = block_diag(c_ref[rows]) + d(m_hi, s_hi) + d(m_lo, s_hi) + d(m_hi, s_lo)
                y_s[b, pl.ds(r0, L), p * P:(p + 1) * P] = y
        return carry

    lax.fori_loop(0, nchunk, chunk_body, 0)

    inv = 1.0 / HEAD_DIM
    for b in range(bsz):
        y = y_s[b]
        mean = _head_sum(y, blk_ref) * inv
        dlt = y - mean
        var = _head_sum(dlt * dlt, blk_ref) * inv
        yn = dlt * lax.rsqrt(var + RW_GN_EPS) * gnw_ref[...] + gnb_ref[...]
        o_ref[b] = ((yn + bonus_ref[b]) * g_ref[b]).astype(o_ref.dtype)


def _rwkv_mix(p_rw, prm, v_first, vres, tt):
    bsz, t, cols = p_rw.shape
    W = WIDTH
    has_vres = vres is not None
    row = lambda a: a.reshape(1, -1).astype(F32)
    full = lambda shape: pl.BlockSpec(shape, lambda b, i: (0,) * len(shape))
    tile = lambda c: pl.BlockSpec((1, tt, c), lambda b, i: (b, i, 0))
    blk = (jnp.arange(W)[:, None] // HEAD_DIM == jnp.arange(W)[None, :] // HEAD_DIM).astype(BF16)

    zpad = jnp.zeros((RW_LORA_PAD // 2, W), F32)
    wup = jnp.concatenate([prm['w_up'], zpad], axis=0).astype(BF16)
    aup = jnp.concatenate([zpad, prm['a_up']], axis=0).astype(BF16)
    ins = [p_rw, p_rw, row(prm['mu']), wup, row(prm['w0']), aup, row(prm['a0']),
           prm['g_up'].astype(BF16), row(prm['k_k']), row(prm['k_a']), row(prm['r_k']), blk]
    nprev = SUBLANES * 4 // p_rw.dtype.itemsize
    in_specs = [tile(cols),
                pl.BlockSpec((1, nprev, cols), lambda b, i: (b, jnp.maximum(i * (tt // nprev) - 1, 0), 0)),
                full((1, cols)), full((RW_LORA_PAD, W)), full((1, W)), full((RW_LORA_PAD, W)),
                full((1, W)), full((RW_LORA_PAD, W)), full((1, W)), full((1, W)), full((1, W)),
                full((W, W))]
    if has_vres:
        v_down, v_up, v0 = vres
        pad = LANES - RW_MV_LORA
        ins += [v_first, jnp.pad(v_down, ((0, 0), (0, pad))).astype(BF16),
                jnp.pad(v_up, ((0, pad), (0, 0))).astype(BF16), row(v0)]
        in_specs += [tile(W), full((W, LANES)), full((LANES, W)), full((1, W))]

    hm = jax.ShapeDtypeStruct((bsz, HEADS // 2, t, 2 * HEAD_DIM), F32)
    tw = jax.ShapeDtypeStruct((bsz, t, W), F32)
    hm_spec = pl.BlockSpec((1, HEADS // 2, tt, 2 * HEAD_DIM), lambda b, i: (b, 0, i, 0))
    out_shape = [hm, hm, hm, hm, tw, tw]
    out_specs = [hm_spec, hm_spec, hm_spec, hm_spec, tile(W), tile(W)]
    if not has_vres:
        out_shape.append(tw)
        out_specs.append(tile(W))
    outs = pl.pallas_call(
        functools.partial(_rwkv_a_kernel, tt=tt, has_vres=has_vres),
        grid=(bsz, t // tt),
        in_specs=in_specs, out_specs=out_specs, out_shape=out_shape,
        compiler_params=_cparams("parallel", "parallel"),
        name="rwkv_a",
    )(*ins)
    if has_vres:
        rq, y0, m, c, bonus, g = outs
    else:
        rq, y0, m, c, bonus, g, v_first = outs

    full1 = lambda shape: pl.BlockSpec(shape, lambda i: (0,) * len(shape))
    tile1 = pl.BlockSpec((bsz, tt, W), lambda i: (0, i, 0))
    hm1 = pl.BlockSpec((bsz, HEADS // 2, tt, 2 * HEAD_DIM), lambda i: (0, 0, i, 0))
    y = pl.pallas_call(
        functools.partial(_rwkv_b_kernel, tt=tt),
        grid=(t // tt,),
        in_specs=[hm1, hm1, hm1, hm1, tile1, tile1, full1((1, W)), full1((1, W)), full1((W, W))],
        out_specs=tile1,
        out_shape=jax.ShapeDtypeStruct((bsz, t, W), BF16),
        scratch_shapes=[pltpu.VMEM((bsz * HEADS // 2, 2 * HEAD_DIM, 2 * HEAD_DIM), F32),
                        pltpu.VMEM((bsz, tt, W), F32)],
        compiler_params=_cparams("arbitrary"),
        name="rwkv_b",
    )(rq, y0, m, c, bonus, g, row(prm['gn_w']), row(prm['gn_b']), blk)
    return y, v_first


def _ssm_kernel(p_ref, pprev_ref, cw_ref, cb_ref, dtb_ref, alog_ref, dsk_ref, nw_ref, exp_ref,
                tri_ref, o_ref, st_s):
    for b in range(p_ref.shape[0]):
        _ssm_one(b, p_ref, pprev_ref, cw_ref, cb_ref, dtb_ref, alog_ref, dsk_ref, nw_ref, exp_ref, tri_ref,
                 o_ref, st_s)


def _ssm_one(b, p_ref, pprev_ref, cw_ref, cb_ref, dtb_ref, alog_ref, dsk_ref, nw_ref, exp_ref, tri_ref,
             o_ref, st_s):
    Q = SSM_CHUNK
    W = WIDTH
    nb = SSM_GROUPS * SSM_STATE
    gw = W // SSM_GROUPS

    @pl.when(pl.program_id(0) == 0)
    def _():
        st_s[b] = jnp.zeros(st_s.shape[1:], F32)

    p = p_ref[b]
    z = p[:, 0:W]
    xbc_in = p[:, W:2 * W + 2 * nb]
    dt_in = p[:, 2 * W + 2 * nb:]
    prev = jnp.where(pl.program_id(0) == 0, 0.0, pprev_ref[b][:, W:2 * W + 2 * nb])
    row = lax.broadcasted_iota(jnp.int32, (Q, 1), 0)
    acc = cb_ref[...] + cw_ref[SSM_CONV - 1:SSM_CONV, :] * xbc_in
    for j in range(1, SSM_CONV):
        sh = pltpu.roll(xbc_in, j, axis=0)
        for i in range(j):
            sh = jnp.where(row == i, prev[SUBLANES - j + i:SUBLANES - j + i + 1, :], sh)
        acc = acc + cw_ref[SSM_CONV - 1 - j:SSM_CONV - j, :] * sh
    xbc = _silu(acc)
    xs = xbc[:, 0:W]
    bm = xbc[:, W:W + nb]
    cm = xbc[:, W + nb:]

    dt = _softplus(dt_in + dtb_ref[...])
    a = -jnp.exp(alog_ref[...]) * dt
    a_cs = _dot_exact_lhs(tri_ref[...], a)
    a_cs_t = a_cs.T
    dt_e = _dot_exact_rhs(dt, exp_ref[...])
    acs_e = _dot_exact_rhs(a_cs, exp_ref[...])
    last_e = acs_e[Q - 1:Q, :]
    xdt = xs * dt_e
    xdec = xdt * jnp.exp(last_e - acs_e)

    li = lax.broadcasted_iota(jnp.int32, (Q, Q), 0)
    si = lax.broadcasted_iota(jnp.int32, (Q, Q), 1)
    causal = li >= si
    hg = HEADS // SSM_GROUPS
    ys = []
    for g in range(SSM_GROUPS):
        bg = bm[:, g * SSM_STATE:(g + 1) * SSM_STATE]
        cg = cm[:, g * SSM_STATE:(g + 1) * SSM_STATE]
        cb = _dot_nt(cg, bg)
        for hh in range(hg):
            h = g * hg + hh
            seg = a_cs[:, h:h + 1] - a_cs_t[h:h + 1, :]
            lmat = jnp.where(causal, jnp.exp(jnp.where(causal, seg, 0.0)), 0.0)
            ys.append(_dot(cb * lmat, xdt[:, h * HEAD_DIM:(h + 1) * HEAD_DIM]))
        st = st_s[b, :, g * gw:(g + 1) * gw]
        y_off = _dot(cg, st) * jnp.exp(acs_e[:, g * gw:(g + 1) * gw])
        ys.append(y_off)
        st_s[b, :, g * gw:(g + 1) * gw] = (jnp.exp(last_e[:, g * gw:(g + 1) * gw]) * st
                                           + _dot_tn(bg, xdec[:, g * gw:(g + 1) * gw]))
    n5 = hg + 1
    y = jnp.concatenate(
        [jnp.concatenate(ys[g * n5:g * n5 + hg], axis=1) + ys[g * n5 + hg] for g in range(SSM_GROUPS)],
        axis=1)
    y = (y + dsk_ref[...] * xs) * _silu(z)
    outs = []
    for g in range(SSM_GROUPS):
        yg = y[:, g * gw:(g + 1) * gw]
        outs.append(yg * lax.rsqrt(jnp.mean(yg * yg, axis=-1, keepdims=True) + RMS_EPS))
    o_ref[b] = (jnp.concatenate(outs, axis=1) * nw_ref[...]).astype(o_ref.dtype)


def _ssm_mix(p_ssm, prm):
    bsz, t, cols = p_ssm.shape
    W = WIDTH
    Q = SSM_CHUNK
    cdim = W + 2 * SSM_GROUPS * SSM_STATE
    row = lambda a: a.reshape(1, -1).astype(F32)
    padrow = lambda a: jnp.pad(a.astype(F32), (0, LANES - a.shape[0])).reshape(1, LANES)
    full = lambda shape: pl.BlockSpec(shape, lambda i: (0,) * len(shape))
    expand = (jnp.arange(LANES)[:, None] == jnp.arange(W)[None, :] // HEAD_DIM).astype(BF16)
    tri = (jnp.arange(Q)[:, None] >= jnp.arange(Q)[None, :]).astype(BF16)
    return pl.pallas_call(
        _ssm_kernel,
        grid=(t // Q,),
        in_specs=[pl.BlockSpec((bsz, Q, cols), lambda i: (0, i, 0)),
                  pl.BlockSpec((bsz, SUBLANES, cols),
                               lambda i: (0, jnp.maximum(i * (Q // SUBLANES) - 1, 0), 0)),
                  full((SSM_CONV, cdim)), full((1, cdim)), full((1, LANES)), full((1, LANES)),
                  full((1, W)), full((1, W)), full((LANES, W)), full((Q, Q))],
        out_specs=pl.BlockSpec((bsz, Q, W), lambda i: (0, i, 0)),
        out_shape=jax.ShapeDtypeStruct((bsz, t, W), BF16),
        scratch_shapes=[pltpu.VMEM((bsz, SSM_STATE, W), F32)],
        compiler_params=_cparams("arbitrary"),
        name="ssm",
    )(p_ssm, p_ssm, prm['conv_w'].T.astype(F32), row(prm['conv_b']), padrow(prm['dt_bias']),
      padrow(prm['a_log']), row(jnp.repeat(prm['d'], HEAD_DIM)), row(prm['norm_w']), expand, tri)


def _ret_log_gamma(h):
    return math.log1p(-(2.0 ** (-5.0 - h)))


def _ret_kernel(p_ref, freq_ref, sgn_ref, blk_ref, o_ref, st_s, dec_s):
    Q = RET_CHUNK
    W = WIDTH
    half = HEAD_DIM // 2
    bsz = p_ref.shape[0]

    @pl.when(pl.program_id(0) == 0)
    def _():
        st_s[...] = jnp.zeros_like(st_s)
        li = lax.broadcasted_iota(jnp.int32, (Q, Q), 0)
        si = lax.broadcasted_iota(jnp.int32, (Q, Q), 1)
        rel = (li - si).astype(F32)
        for h in range(HEADS):
            dec_s[h] = jnp.where(li >= si, jnp.exp(jnp.where(li >= si, rel, 0.0) * _ret_log_gamma(h)), 0.0)

    idx = lax.broadcasted_iota(jnp.int32, (Q, 1), 0)
    pos = (pl.program_id(0) * Q + idx).astype(F32)
    ang = pos * freq_ref[...]
    cos = jnp.concatenate([jnp.cos(ang)] * (W // LANES), axis=1)
    sin = jnp.concatenate([jnp.sin(ang) * sgn_ref[...]] * (W // LANES), axis=1)
    lane = lax.broadcasted_iota(jnp.int32, (1, W), 1)
    first_half = (lane & (HEAD_DIM - 1)) < half

    def rot(x):
        partner = jnp.where(first_half, pltpu.roll(x, W - half, axis=1), pltpu.roll(x, half, axis=1))
        return x * cos + partner * sin

    def heads(x):
        return jnp.stack([x[:, h * HEAD_DIM:(h + 1) * HEAD_DIM] for h in range(HEADS)], axis=0)

    def per_head(fn):
        return jnp.concatenate([jnp.stack([fn(_ret_log_gamma(h)) for h in range(HEADS)], axis=0)] * bsz, axis=0)

    idf = idx.astype(F32)
    xi = per_head(lambda lg: jnp.exp((idf + 1.0) * lg))
    zeta = per_head(lambda lg: jnp.exp((Q - 1.0 - idf) * lg))
    cdec = per_head(lambda lg: jnp.full((1, 1), math.exp(Q * lg), F32))
    dec = jnp.concatenate([dec_s[...]] * bsz, axis=0)

    ps = [p_ref[b].astype(F32) for b in range(bsz)]
    q = jnp.concatenate([heads(rot(p[:, 0:W])) for p in ps], axis=0)
    k = jnp.concatenate([heads(rot(p[:, W:2 * W]) * (HEAD_DIM ** -0.5)) for p in ps], axis=0)
    v = jnp.concatenate([heads(p[:, 2 * W:3 * W]) for p in ps], axis=0)
    st = st_s[...]
    y = _bmm(_bmm_nt(q, k) * dec, v) + _bmm(q, st) * xi
    st_s[...] = cdec * st + _bmm_tn(k * zeta, v)
    for b in range(bsz):
        yb = jnp.concatenate([y[b * HEADS + h] for h in range(HEADS)], axis=1)
        ms = _head_sum(yb * yb, blk_ref) * (1.0 / HEAD_DIM)
        o_ref[b] = (yb * lax.rsqrt(ms + RMS_EPS) * _silu(ps[b][:, 3 * W:])).astype(o_ref.dtype)


def _ret_mix(p_ret):
    bsz, t, cols = p_ret.shape
    W = WIDTH
    Q = RET_CHUNK
    half = HEAD_DIM // 2
    full = lambda shape: pl.BlockSpec(shape, lambda i: (0,) * len(shape))
    inv_freq = ROPE_BASE ** (-jnp.arange(half, dtype=F32) / half)
    freq = jnp.tile(inv_freq, LANES // half).reshape(1, LANES)
    sgn = jnp.where((jnp.arange(LANES) % HEAD_DIM) < half, -1.0, 1.0).astype(F32).reshape(1, LANES)
    blk = (jnp.arange(W)[:, None] // HEAD_DIM == jnp.arange(W)[None, :] // HEAD_DIM).astype(BF16)
    return pl.pallas_call(
        _ret_kernel,
        grid=(t // Q,),
        in_specs=[pl.BlockSpec((bsz, Q, cols), lambda i: (0, i, 0)),
                  full((1, LANES)), full((1, LANES)), full((W, W))],
        out_specs=pl.BlockSpec((bsz, Q, W), lambda i: (0, i, 0)),
        out_shape=jax.ShapeDtypeStruct((bsz, t, W), BF16),
        scratch_shapes=[pltpu.VMEM((bsz * HEADS, HEAD_DIM, HEAD_DIM), F32), pltpu.VMEM((HEADS, Q, Q), F32)],
        compiler_params=_cparams("arbitrary"),
        name="retention",
    )(p_ret, freq, sgn, blk)


def _merge_kernel(x_ref, gl_ref, y0_ref, y1_ref, y2_ref, wb_ref, wo_ref, o_ref):
    d = x_ref.shape[1]
    merged = None
    for i, y_ref in enumerate((y0_ref, y1_ref, y2_ref)):
        gate = _sigmoid(gl_ref[:, i * d:(i + 1) * d].astype(F32))
        term = gate * jnp.dot(y_ref[...], wb_ref[i], preferred_element_type=F32)
        merged = term if merged is None else merged + term
    o_ref[...] = x_ref[...] + _dot(merged, wo_ref[...])


def _merge(x2, gate_logits, ys, w_branch, w_out, tm):
    n, d = x2.shape
    W = WIDTH
    rows = lambda c: pl.BlockSpec((tm, c), lambda i: (i, 0))
    return pl.pallas_call(
        _merge_kernel,
        grid=(n // tm,),
        in_specs=[rows(d), rows(3 * d), rows(W), rows(W), rows(W),
                  pl.BlockSpec((3, W, d), lambda i: (0, 0, 0)),
                  pl.BlockSpec((d, d), lambda i: (0, 0))],
        out_specs=rows(d),
        out_shape=jax.ShapeDtypeStruct((n, d), F32),
        compiler_params=_cparams("parallel"),
        name="merge_out",
    )(x2, gate_logits, *ys, w_branch.astype(BF16), w_out.astype(BF16))


def _ffn_kernel(x_ref, nw_ref, wg_ref, wu_ref, wd_ref, fw_ref, o_ref, h_s, acc_s, *, final_norm):
    j = pl.program_id(1)

    @pl.when(j == 0)
    def _():
        h_s[...] = _rms_rows(x_ref[...], nw_ref[...]).astype(BF16)
        acc_s[...] = jnp.zeros_like(acc_s)

    h = h_s[...]
    act = _silu(jnp.dot(h, wg_ref[...], preferred_element_type=F32)) * jnp.dot(
        h, wu_ref[...], preferred_element_type=F32)
    acc_s[...] += _dot(act, wd_ref[...])

    @pl.when(j == pl.num_programs(1) - 1)
    def _():
        y = x_ref[...] + acc_s[...]
        if final_norm:
            y = _rms_rows(y, fw_ref[...])
        o_ref[...] = y


def _ffn(x2, norm_w, wg, wu, wd, final_w, tm, tf):
    n, d = x2.shape
    f = wg.shape[1]
    final_norm = final_w is not None
    fw = (final_w if final_norm else jnp.ones((d,), F32)).reshape(1, d)
    return pl.pallas_call(
        functools.partial(_ffn_kernel, final_norm=final_norm),
        grid=(n // tm, f // tf),
        in_specs=[pl.BlockSpec((tm, d), lambda i, j: (i, 0)), pl.BlockSpec((1, d), lambda i, j: (0, 0)),
                  pl.BlockSpec((d, tf), lambda i, j: (0, j)), pl.BlockSpec((d, tf), lambda i, j: (0, j)),
                  pl.BlockSpec((tf, d), lambda i, j: (j, 0)), pl.BlockSpec((1, d), lambda i, j: (0, 0))],
        out_specs=pl.BlockSpec((tm, d), lambda i, j: (i, 0)),
        out_shape=jax.ShapeDtypeStruct((n, d), F32),
        scratch_shapes=[pltpu.VMEM((tm, d), BF16), pltpu.VMEM((tm, d), F32)],
        compiler_params=_cparams("parallel", "arbitrary"),
        name="ffn",
    )(x2, norm_w.reshape(1, d), wg, wu, wd, fw)


def _router_kernel(x_ref, nw_ref, rt_ref, h_ref, gate_ref, pos_ref, post_ref, cnt_ref, *, n_experts):
    tm = x_ref.shape[0]
    h = _rms_rows(x_ref[...], nw_ref[...])
    h_ref[...] = h.astype(BF16)
    r_hi, r_lo = rt_ref[0], rt_ref[1]
    h_hi, h_mid, h_lo = _split3(h)
    d = lambda a, b: jnp.dot(a, b, preferred_element_type=F32)
    logits = d(h_hi, r_hi) + d(h_mid, r_hi) + d(h_hi, r_lo) + d(h_lo, r_hi) + d(h_mid, r_lo)
    lane = lax.broadcasted_iota(jnp.int32, logits.shape, 1)
    neg = jnp.float32(-jnp.inf)
    logits = jnp.where(lane < n_experts, logits, neg)
    m1 = jnp.max(logits, axis=-1, keepdims=True)
    i1 = jnp.min(jnp.where(logits == m1, lane, LANES), axis=-1, keepdims=True)
    rest = jnp.where(lane == i1, neg, logits)
    m2 = jnp.max(rest, axis=-1, keepdims=True)
    i2 = jnp.min(jnp.where(rest == m2, lane, LANES), axis=-1, keepdims=True)
    e2 = jnp.exp(m2 - m1)
    w1 = 1.0 / (1.0 + e2)
    gate_ref[...] = jnp.where(lane == i1, w1, 0.0) + jnp.where(lane == i2, e2 * w1, 0.0)
    member = (lane == i1) | (lane == i2)
    m = jnp.where(member, 1.0, 0.0)
    row = lax.broadcasted_iota(jnp.int32, (tm, 1), 0)
    c = m
    step = 1
    while step < tm:
        c = c + jnp.where(row >= step, pltpu.roll(c, step, axis=0), 0.0)
        step *= 2
    posm = jnp.where(member, c - m, -1.0)
    pos_ref[...] = posm
    post_ref[0] = posm.T[0:SUBLANES, :]
    cnt_ref[0] = c[tm - 1:tm, :].astype(jnp.int32)


def _moe_gather_kernel(gi, ge, gr, gs, gf, h_ref, post_ref, xs_init_ref, xs_ref, *, br):
    w = pl.program_id(0)
    flags = gf[w]

    @pl.when((flags & 1) == 1)
    def _():
        rank = post_ref[0, pl.ds(ge[w], 1), :]
        slot = jnp.where(rank >= 0.0, rank + gs[w].astype(F32), -1.0)
        rows_i = lax.broadcasted_iota(jnp.int32, (br, 1), 0).astype(F32)
        sel = jnp.where(slot == rows_i, 1.0, 0.0).astype(BF16)
        rows = jnp.dot(sel, h_ref[...], preferred_element_type=F32).astype(BF16)

        @pl.when((flags & 2) == 2)
        def _():
            xs_ref[...] = rows

        @pl.when((flags & 2) == 0)
        def _():
            xs_ref[...] = (xs_ref[...].astype(F32) + rows.astype(F32)).astype(BF16)


def _moe_expert_kernel(eid, act, x_ref, wg_ref, wu_ref, wd_ref, o_ref, acc_s):
    r = pl.program_id(0)
    j = pl.program_id(1)
    last = j == pl.num_programs(1) - 1

    @pl.when(act[r] == 1)
    def _():
        x = x_ref[...]
        a = _silu(jnp.dot(x, wg_ref[0], preferred_element_type=F32)) * jnp.dot(
            x, wu_ref[0], preferred_element_type=F32)
        y = _dot(a, wd_ref[0])

        @pl.when(j == 0)
        def _():
            acc_s[...] = y

        @pl.when(j > 0)
        def _():
            acc_s[...] += y

        @pl.when(last)
        def _():
            o_ref[...] = acc_s[...].astype(o_ref.dtype)

    @pl.when((act[r] == 0) & last)
    def _():
        o_ref[...] = jnp.zeros_like(o_ref)


def _moe_combine_kernel(ci, ce, cr, cs, cf, x_ref, gate_ref, pos_ref, ys_ref, fw_ref, o_ref, *, br, final_norm):
    w = pl.program_id(0)
    flags = cf[w]
    tm = x_ref.shape[0]

    @pl.when((flags & 1) == 1)
    def _():
        @pl.when((flags & 2) == 2)
        def _():
            o_ref[...] = x_ref[...]

        lane = lax.broadcasted_iota(jnp.int32, (tm, LANES), 1)
        pick = lambda ref: jnp.sum(jnp.where(lane == ce[w], ref[...], 0.0), axis=-1, keepdims=True)
        rank = pick(pos_ref)
        slot = jnp.where(rank >= 0.0, rank + cs[w].astype(F32), -1.0)
        cols_i = lax.broadcasted_iota(jnp.int32, (1, br), 1).astype(F32)
        sel_t = jnp.where(slot == cols_i, 1.0, 0.0).astype(BF16)
        o_ref[...] += pick(gate_ref) * jnp.dot(sel_t, ys_ref[...], preferred_element_type=F32)

        if final_norm:
            @pl.when((flags & 4) == 4)
            def _():
                o_ref[...] = _rms_rows(o_ref[...], fw_ref[...])


def _moe_plan(cnt, tm, br, rm):
    nt, n_e = cnt.shape
    s_max = (-(-nt * tm * 2 // rm) + n_e) * rm
    tot = cnt.sum(0)
    ptot = (tot + rm - 1) // rm * rm
    seg_end = jnp.cumsum(ptot)
    off = (seg_end - ptot)[None, :] + jnp.cumsum(cnt, 0) - cnt
    first_blk = off // br
    nb = jnp.where(cnt > 0, (off + cnt - 1) // br - first_blk + 1, 0)
    w_max = s_max // br + nt * n_e
    w = jnp.arange(w_max, dtype=jnp.int32)

    def work_list(order):
        nb_o = nb.reshape(-1)[order]
        incl = jnp.cumsum(nb_o)
        total = incl[-1]
        w_eff = jnp.minimum(w, total - 1)
        p = jnp.clip(jnp.searchsorted(incl, w_eff, side='right'), 0, nt * n_e - 1)
        pair = order[p]
        i_w, e_w = pair // n_e, pair % n_e
        r_w = first_blk.reshape(-1)[pair] + (w_eff - (incl - nb_o)[p])
        shift = off.reshape(-1)[pair] - r_w * br
        return (i_w, e_w, r_w, shift, w < total)

    ids = jnp.arange(nt * n_e, dtype=jnp.int32)
    gi, ge, gr, gs, gv = work_list(ids.reshape(nt, n_e).T.reshape(-1))
    g_first = gv & jnp.concatenate([jnp.ones((1,), bool), gr[1:] != gr[:-1]])
    g_flags = gv.astype(jnp.int32) + 2 * g_first.astype(jnp.int32)
    ci, ce, cr, cs, cv = work_list(ids)
    c_first = cv & jnp.concatenate([jnp.ones((1,), bool), ci[1:] != ci[:-1]])
    nxt_valid = jnp.concatenate([cv[1:], jnp.zeros((1,), bool)])
    c_last = cv & (jnp.concatenate([ci[1:] != ci[:-1], jnp.ones((1,), bool)]) | ~nxt_valid)
    c_flags = cv.astype(jnp.int32) + 2 * c_first.astype(jnp.int32) + 4 * c_last.astype(jnp.int32)
    r0 = jnp.arange(s_max // rm, dtype=jnp.int32) * rm
    eid = jnp.clip(jnp.searchsorted(seg_end, r0, side='right'), 0, n_e - 1)
    act = (r0 < seg_end[-1]).astype(jnp.int32)
    i32 = lambda *xs: tuple(x.astype(jnp.int32) for x in xs)
    return s_max, i32(gi, ge, gr, gs, g_flags), i32(ci, ce, cr, cs, c_flags), i32(eid, act)


def _moe(x2, norm_w, wg, wu, wd, router, final_w, tm, tf, br, rm):
    n, d = x2.shape
    n_experts, _, f = wg.shape
    nt = n // tm
    final_norm = final_w is not None
    fw = (final_w if final_norm else jnp.ones((d,), F32)).reshape(1, d)
    rt = jnp.pad(router.astype(F32), ((0, 0), (0, LANES - n_experts)))
    r_hi = rt.astype(BF16)
    r_lo = (rt - r_hi.astype(F32)).astype(BF16)
    rows = lambda c: pl.BlockSpec((tm, c), lambda i: (i, 0))
    h, gates, pos, pos_t, cnt = pl.pallas_call(
        functools.partial(_router_kernel, n_experts=n_experts),
        grid=(nt,),
        in_specs=[rows(d), pl.BlockSpec((1, d), lambda i: (0, 0)),
                  pl.BlockSpec((2, d, LANES), lambda i: (0, 0, 0))],
        out_specs=[rows(d), rows(LANES), rows(LANES),
                   pl.BlockSpec((1, SUBLANES, tm), lambda i: (i, 0, 0)),
                   pl.BlockSpec((1, 1, LANES), lambda i: (i, 0, 0))],
        out_shape=[jax.ShapeDtypeStruct((n, d), BF16), jax.ShapeDtypeStruct((n, LANES), F32),
                   jax.ShapeDtypeStruct((n, LANES), F32), jax.ShapeDtypeStruct((nt, SUBLANES, tm), F32),
                   jax.ShapeDtypeStruct((nt, 1, LANES), jnp.int32)],
        compiler_params=_cparams("parallel"),
        name="moe_router",
    )(x2, norm_w.reshape(1, d), jnp.stack([r_hi, r_lo]))
    s_max, g_meta, c_meta, e_meta = _moe_plan(cnt[:, 0, :n_experts], tm, br, rm)
    n_work = g_meta[0].shape[0]

    xs = pl.pallas_call(
        functools.partial(_moe_gather_kernel, br=br),
        grid_spec=pltpu.PrefetchScalarGridSpec(
            num_scalar_prefetch=5, grid=(n_work,),
            in_specs=[pl.BlockSpec((tm, d), lambda w, gi, ge, gr, gs, gf: (gi[w], 0)),
                      pl.BlockSpec((1, SUBLANES, tm), lambda w, gi, ge, gr, gs, gf: (gi[w], 0, 0)),
                      pl.BlockSpec(memory_space=pl.ANY)],
            out_specs=pl.BlockSpec((br, d), lambda w, gi, ge, gr, gs, gf: (gr[w], 0))),
        out_shape=jax.ShapeDtypeStruct((s_max, d), BF16),
        input_output_aliases={7: 0},
        compiler_params=_cparams("arbitrary"),
        name="moe_gather",
    )(*g_meta, h, pos_t, jnp.zeros((s_max, d), BF16))

    nj = f // tf
    jj = lambda j, r, act: j * act[r] + (nj - 1) * (1 - act[r])
    ys = pl.pallas_call(
        _moe_expert_kernel,
        grid_spec=pltpu.PrefetchScalarGridSpec(
            num_scalar_prefetch=2, grid=(s_max // rm, nj),
            in_specs=[pl.BlockSpec((rm, d), lambda r, j, eid, act: (r, 0)),
                      pl.BlockSpec((1, d, tf), lambda r, j, eid, act: (eid[r], 0, jj(j, r, act))),
                      pl.BlockSpec((1, d, tf), lambda r, j, eid, act: (eid[r], 0, jj(j, r, act))),
                      pl.BlockSpec((1, tf, d), lambda r, j, eid, act: (eid[r], jj(j, r, act), 0))],
            out_specs=pl.BlockSpec((rm, d), lambda r, j, eid, act: (r, 0)),
            scratch_shapes=[pltpu.VMEM((rm, d), F32)]),
        out_shape=jax.ShapeDtypeStruct((s_max, d), BF16),
        compiler_params=_cparams("parallel", "arbitrary"),
        name="moe_experts",
    )(*e_meta, xs, wg, wu, wd)

    tile = lambda c: pl.BlockSpec((tm, c), lambda w, ci, ce, cr, cs, cf: (ci[w], 0))
    return pl.pallas_call(
        functools.partial(_moe_combine_kernel, br=br, final_norm=final_norm),
        grid_spec=pltpu.PrefetchScalarGridSpec(
            num_scalar_prefetch=5, grid=(n_work,),
            in_specs=[tile(d), tile(LANES), tile(LANES),
                      pl.BlockSpec((br, d), lambda w, ci, ce, cr, cs, cf: (cr[w], 0)),
                      pl.BlockSpec((1, d), lambda w, ci, ce, cr, cs, cf: (0, 0))],
            out_specs=tile(d)),
        out_shape=jax.ShapeDtypeStruct((n, d), F32),
        compiler_params=_cparams("arbitrary"),
        name="moe_combine",
    )(*c_meta, x2, gates, pos, ys, fw)


def _tiles(n_rows, t):
    tm = 512 if n_rows % 512 == 0 else 256
    tt = 256 if t % 256 == 0 else RW_CHUNK
    tm_moe = 1024 if n_rows % 1024 == 0 else tm
    br, rm = 256, 512
    return dict(tm=tm, tt=tt, tm_moe=tm_moe, br=br, rm=rm)


def _ff_tile(f, max_tile=2048):
    for parts in (2, 4, 7, 11, 14, 22, 28):
        if f % parts == 0 and (f // parts) % LANES == 0 and f // parts <= max_tile:
            return f // parts
    return f


def _deinterleave_heads(w):
    d_in = w.shape[0]
    return w.reshape(d_in, HEADS, HEAD_DIM // 2, 2).transpose(0, 1, 3, 2).reshape(d_in, WIDTH)


def kernel(x, norm1_w, w_in, rw_mu, rw_w_up, rw_w0, rw_a_up, rw_a0, rw_g_up, rw_k_k, rw_k_a, rw_r_k, rw_gn_w, rw_gn_b, rw_v_down, rw_v_up, rw_v0, ssm_conv_w, ssm_conv_b, ssm_dt_bias, ssm_a_log, ssm_d, ssm_norm_w, w_branch, w_out, norm2_w, ffn_wg, ffn_wu, ffn_wd, moe_router, moe_wg, moe_wu, moe_wd, final_norm_w):
    bsz, t, d = x.shape
    depth = w_in.shape[0]
    n = bsz * t
    W = WIDTH
    tl = _tiles(n, t)
    rw_cols = 3 * W + rw_w_up.shape[1] + rw_a_up.shape[1] + rw_g_up.shape[1]
    cdim = ssm_conv_w.shape[1]
    ssm_cols = W + cdim + ssm_a_log.shape[1]
    c1 = rw_cols
    c2 = c1 + ssm_cols
    c3 = c2 + 4 * W
    ssm_pad = (-ssm_cols) % LANES

    x2 = x.reshape(n, d)
    v_first = None
    for layer in range(depth):
        wl = w_in[layer]
        w_rw = wl[:, :c1].astype(BF16)
        w_ssm = jnp.pad(wl[:, c1:c2], ((0, 0), (0, ssm_pad))).astype(BF16)
        w_ret = jnp.concatenate([_deinterleave_heads(wl[:, c2:c2 + W]),
                                 _deinterleave_heads(wl[:, c2 + W:c2 + 2 * W]),
                                 wl[:, c2 + 2 * W:c3]], axis=1).astype(BF16)
        w_gate = wl[:, c3:].astype(BF16)
        p_rw = _norm_proj(x2, norm1_w[layer], w_rw, tl['tm'], BF16).reshape(bsz, t, -1)
        p_ssm = _norm_proj(x2, norm1_w[layer], w_ssm, tl['tm']).reshape(bsz, t, -1)
        p_ret = _norm_proj(x2, norm1_w[layer], w_ret, tl['tm'], BF16).reshape(bsz, t, -1)
        gate_logits = _norm_proj(x2, norm1_w[layer], w_gate, tl['tm'], BF16)

        rw_prm = dict(mu=rw_mu[layer], w_up=rw_w_up[layer], w0=rw_w0[layer], a_up=rw_a_up[layer],
                      a0=rw_a0[layer], g_up=rw_g_up[layer], k_k=rw_k_k[layer], k_a=rw_k_a[layer],
                      r_k=rw_r_k[layer].reshape(-1), gn_w=rw_gn_w[layer], gn_b=rw_gn_b[layer])
        vres = None if layer == 0 else (rw_v_down[layer - 1], rw_v_up[layer - 1], rw_v0[layer - 1])
        y_rw, v_first = _rwkv_mix(p_rw, rw_prm, v_first, vres, tl['tt'])
        ssm_prm = dict(conv_w=ssm_conv_w[layer], conv_b=ssm_conv_b[layer], dt_bias=ssm_dt_bias[layer],
                       a_log=ssm_a_log[layer], d=ssm_d[layer], norm_w=ssm_norm_w[layer])
        y_ssm = _ssm_mix(p_ssm, ssm_prm)
        y_ret = _ret_mix(p_ret)
        ys = [y.reshape(n, W) for y in (y_rw, y_ssm, y_ret)]
        x2 = _merge(x2, gate_logits, ys, w_branch[layer], w_out[layer], tl['tm'])

        j = layer // 2
        final_w = final_norm_w if layer == depth - 1 else None
        if layer % 2 == 0:
            x2 = _ffn(x2, norm2_w[layer], ffn_wg[j].astype(BF16), ffn_wu[j].astype(BF16),
                      ffn_wd[j].astype(BF16), final_w, tl['tm'], _ff_tile(ffn_wg.shape[2]))
        else:
            x2 = _moe(x2, norm2_w[layer], moe_wg[j].astype(BF16), moe_wu[j].astype(BF16),
                      moe_wd[j].astype(BF16), moe_router[j], final_w, tl['tm_moe'],
                      _ff_tile(moe_wg.shape[3], 1024), tl['br'], tl['rm'])
    return x2.reshape(bsz, t, d)
```

```python
import functools
import math

import jax
import jax.numpy as jnp
from jax import lax
from jax.experimental import pallas as pl
from jax.experimental.pallas import tpu as pltpu

F32 = jnp.float32
BF16 = jnp.bfloat16

LANES = 128
SUBLANES = 8
VMEM_LIMIT_BYTES = 56 * 1024 * 1024

RMS_EPS = 1e-6
RW_GN_EPS = 64e-5
HEADS = 8
HEAD_DIM = 64
WIDTH = HEADS * HEAD_DIM
RW_LORA_PAD = 128
RW_MV_LORA = 32
SSM_GROUPS = 2
SSM_STATE = 128
SSM_CONV = 4
ROPE_BASE = 10000.0
N_EXPERTS = 8

RW_CHUNK = 64
SSM_CHUNK = 128
RET_CHUNK = 128


def _cparams(*sem):
    return pltpu.CompilerParams(dimension_semantics=sem, vmem_limit_bytes=VMEM_LIMIT_BYTES)


def _dot(a, b):
    return jnp.dot(a.astype(BF16), b.astype(BF16), preferred_element_type=F32)


def _dot_nt(a, b):
    return lax.dot_general(a.astype(BF16), b.astype(BF16), (((1,), (1,)), ((), ())),
                           preferred_element_type=F32)


def _dot_tn(a, b):
    return lax.dot_general(a.astype(BF16), b.astype(BF16), (((0,), (0,)), ((), ())),
                           preferred_element_type=F32)


def _bmm(a, b):
    return lax.dot_general(a.astype(BF16), b.astype(BF16), (((2,), (1,)), ((0,), (0,))),
                           preferred_element_type=F32)


def _bmm_nt(a, b):
    return lax.dot_general(a.astype(BF16), b.astype(BF16), (((2,), (2,)), ((0,), (0,))),
                           preferred_element_type=F32)


def _bmm_tn(a, b):
    return lax.dot_general(a.astype(BF16), b.astype(BF16), (((1,), (1,)), ((0,), (0,))),
                           preferred_element_type=F32)


def _split2(x):
    hi = x.astype(BF16)
    lo = (x - hi.astype(F32)).astype(BF16)
    return hi, lo


def _split3(x):
    hi = x.astype(BF16)
    r = x - hi.astype(F32)
    mid = r.astype(BF16)
    lo = (r - mid.astype(F32)).astype(BF16)
    return hi, mid, lo


def _dot_exact_rhs(x, m_bf16):
    hi, mid, lo = _split3(x)
    d = lambda a: jnp.dot(a, m_bf16, preferred_element_type=F32)
    return d(hi) + d(mid) + d(lo)


def _dot_exact_lhs(m_bf16, x):
    hi, mid, lo = _split3(x)
    d = lambda a: jnp.dot(m_bf16, a, preferred_element_type=F32)
    return d(hi) + d(mid) + d(lo)


def _sigmoid(x):
    return 1.0 / (1.0 + jnp.exp(-x))


def _silu(x):
    return x * _sigmoid(x)


def _softplus(x):
    return jnp.maximum(x, 0.0) + jnp.log(1.0 + jnp.exp(-jnp.abs(x)))


def _rms_rows(x, w):
    return x * lax.rsqrt(jnp.mean(x * x, axis=-1, keepdims=True) + RMS_EPS) * w


def _norm_proj_kernel(x_ref, nw_ref, w_ref, o_ref):
    h = _rms_rows(x_ref[...], nw_ref[...]).astype(BF16)
    o_ref[...] = jnp.dot(h, w_ref[...], preferred_element_type=F32).astype(o_ref.dtype)


def _norm_proj(x2, norm_w, w_bf16, tm, out_dtype=F32):
    n, d = x2.shape
    c = w_bf16.shape[1]
    return pl.pallas_call(
        _norm_proj_kernel,
        grid=(n // tm,),
        in_specs=[pl.BlockSpec((tm, d), lambda i: (i, 0)),
                  pl.BlockSpec((1, d), lambda i: (0, 0)),
                  pl.BlockSpec((d, c), lambda i: (0, 0))],
        out_specs=pl.BlockSpec((tm, c), lambda i: (i, 0)),
        out_shape=jax.ShapeDtypeStruct((n, c), out_dtype),
        compiler_params=_cparams("parallel"),
        name="norm_proj",
    )(x2, norm_w.reshape(1, d), w_bf16)


def _head_sum(x, blk_ref):
    hi, lo = _split2(x)
    m = blk_ref[...]
    return jnp.dot(hi, m, preferred_element_type=F32) + jnp.dot(lo, m, preferred_element_type=F32)


def _rwkv_a_kernel(*refs, tt, has_vres):
    if has_vres:
        (p_ref, pprev_ref, mu_ref, wup_ref, w0_ref, aup_ref, a0_ref, gup_ref, kk_ref, ka_ref,
         rk_ref, blk_ref, vfirst_ref, vdown_ref, vup_ref, v0_ref,
         rq_ref, y0_ref, m_ref, c_ref, bonus_ref, g_ref) = refs
        vout_ref = None
    else:
        (p_ref, pprev_ref, mu_ref, wup_ref, w0_ref, aup_ref, a0_ref, gup_ref, kk_ref, ka_ref,
         rk_ref, blk_ref,
         rq_ref, y0_ref, m_ref, c_ref, bonus_ref, g_ref, vout_ref) = refs
    L = RW_CHUNK
    nchunk = tt // L
    W = WIDTH

    p = p_ref[0].astype(F32)
    nprev = pprev_ref.shape[1]
    prev = jnp.where(pl.program_id(1) == 0, 0.0, pprev_ref[0][nprev - 1:nprev, :].astype(F32))
    row = lax.broadcasted_iota(jnp.int32, (tt, 1), 0)
    shifted = jnp.where(row == 0, prev, pltpu.roll(p, 1, axis=0))
    xm = p + (shifted - p) * mu_ref[...]

    r = xm[:, 0:W]
    k = xm[:, W:2 * W]
    v = xm[:, 2 * W:3 * W]
    xwa = xm[:, 3 * W:3 * W + RW_LORA_PAD]
    xg = xm[:, 3 * W + RW_LORA_PAD:]

    ld = -math.exp(-0.5) * _sigmoid(w0_ref[...] + _dot(jnp.tanh(xwa), wup_ref[...]))
    a_sig = _sigmoid(a0_ref[...] + _dot(xwa, aup_ref[...]))
    g_ref[0] = _dot(_sigmoid(xg), gup_ref[...])
    if has_vres:
        vu = _dot(_dot(v, vdown_ref[...]), vup_ref[...])
        v = v + (vfirst_ref[0] - v) * _sigmoid(v0_ref[...] + vu)
    else:
        vout_ref[0] = v

    kk = k * kk_ref[...]
    kk = kk * lax.rsqrt(jnp.maximum(_head_sum(kk * kk, blk_ref), 1e-24))
    k = k * (1.0 + (a_sig - 1.0) * ka_ref[...])
    bonus_ref[0] = _head_sum(r * k * rk_ref[...], blk_ref) * v
    avec = -kk
    bvec = kk * a_sig

    cum = ld
    pos = row & (L - 1)
    d = 1
    while d < L:
        cum = cum + jnp.where(pos >= d, pltpu.roll(cum, d, axis=0), 0.0)
        d *= 2
    cum3 = cum.reshape(nchunk, L, W)
    last = jnp.broadcast_to(cum3[:, L - 1:L, :], (nchunk, L, W)).reshape(tt, W)
    w_inv = jnp.exp(-cum)
    w_last = jnp.exp(last - cum)

    def to_g(val):
        val = val.astype(BF16)
        return jnp.concatenate(
            [val[:, h * HEAD_DIM:(h + 1) * HEAD_DIM].reshape(nchunk, L, HEAD_DIM) for h in range(HEADS)],
            axis=0)

    at = to_g(avec * jnp.exp(cum - ld))
    rt = to_g(r * jnp.exp(cum))
    bt = to_g(bvec * w_inv)
    kt = to_g(k * w_inv)
    bh = to_g(bvec * w_last)
    kh = to_g(k * w_last)
    vv = to_g(v)
    wl = jnp.concatenate(
        [jnp.exp(cum3[:, L - 1:L, h * HEAD_DIM:(h + 1) * HEAD_DIM]) for h in range(HEADS)], axis=0)

    ri = lax.broadcasted_iota(jnp.int32, (1, L, L), 1)
    ci = lax.broadcasted_iota(jnp.int32, (1, L, L), 2)
    strict = ri > ci
    incl = ri >= ci
    eye = ri == ci

    g4 = _bmm_nt(jnp.concatenate([at, rt], axis=1), jnp.concatenate([bt, kt], axis=1))
    nmat = jnp.where(strict, g4[:, 0:L, 0:L], 0.0)
    a_ak = jnp.where(strict, g4[:, 0:L, L:2 * L], 0.0)
    a_rb = jnp.where(incl, g4[:, L:2 * L, 0:L], 0.0)
    a_rk = jnp.where(incl, g4[:, L:2 * L, L:2 * L], 0.0)
    tp = nmat
    npow = nmat
    span = 1
    while span * 2 < L:
        npow = _bmm(npow, npow)
        tp = tp + npow + _bmm(tp, npow)
        span *= 2
    av = _bmm(jnp.concatenate([a_ak, a_rk], axis=1), vv)
    x0 = jnp.concatenate([at.astype(F32), av[:, 0:L]], axis=2)
    x = x0 + _bmm(tp, x0)
    ry = jnp.concatenate([rt.astype(F32), av[:, L:2 * L]], axis=2) + _bmm(a_rb, x)
    bx = _bmm_tn(bh, x)
    kv = _bmm_tn(kh, vv)
    mc = bx + jnp.concatenate([jnp.where(eye, wl, 0.0), kv], axis=2)

    def pairs(z, out_lo, out_hi):
        z = z.reshape(HEADS // 2, 2, tt, 2 * HEAD_DIM)
        za, zb = z[:, 0], z[:, 1]
        low = lax.broadcasted_iota(jnp.int32, (1, 1, 2 * HEAD_DIM), 2) < HEAD_DIM
        out_lo[0] = jnp.where(low, za, pltpu.roll(zb, HEAD_DIM, axis=2))
        out_hi[0] = jnp.where(low, pltpu.roll(za, HEAD_DIM, axis=2), zb)

    pairs(ry, rq_ref, y0_ref)
    pairs(mc, m_ref, c_ref)


def _rwkv_b_kernel(rq_ref, y0_ref, m_ref, c_ref, bonus_ref, g_ref, gnw_ref, gnb_ref, blk_ref, o_ref,
                   st_s, y_s, *, tt):
    L = RW_CHUNK
    nchunk = tt // L
    bsz = rq_ref.shape[0]
    npair = HEADS // 2
    P = 2 * HEAD_DIM

    @pl.when(pl.program_id(0) == 0)
    def _():
        st_s[...] = jnp.zeros_like(st_s)

    ri = lax.broadcasted_iota(jnp.int32, (P, P), 0) < HEAD_DIM
    ci = lax.broadcasted_iota(jnp.int32, (P, P), 1) < HEAD_DIM
    diag_blocks = ri == ci

    def block_diag(z):
        return jnp.where(diag_blocks, jnp.concatenate([z, z], axis=0), 0.0)

    def chunk_body(c, carry):
        r0 = pl.multiple_of(c * L, L)
        d = lambda a, b_: jnp.dot(a, b_, preferred_element_type=F32)
        for b in range(bsz):
            for p in range(npair):
                st = st_s[b * npair + p]
                rows = (b, p, pl.ds(r0, L), slice(None))
                y = y0_ref[rows] + _dot(rq_ref[rows], st)
                m_hi, m_lo = _split2(block_diag(m_ref[rows]))
                s_hi, s_lo = _split2(st)
                st_s[b * npair + p] = block_diag(c_ref[rows]) + d(m_hi, s_hi) + d(m_lo, s_hi) + d(m_hi, s_lo)
                y_s[b, pl.ds(r0, L), p * P:(p + 1) * P] = y
        return carry

    lax.fori_loop(0, nchunk, chunk_body, 0)

    inv = 1.0 / HEAD_DIM
    for b in range(bsz):
        y = y_s[b]
        mean = _head_sum(y, blk_ref) * inv
        dlt = y - mean
        var = _head_sum(dlt * dlt, blk_ref) * inv
        yn = dlt * lax.rsqrt(var + RW_GN_EPS) * gnw_ref[...] + gnb_ref[...]
        o_ref[b] = ((yn + bonus_ref[b]) * g_ref[b]).astype(o_ref.dtype)


def _rwkv_mix(p_rw, prm, v_first, vres, tt):
    bsz, t, cols = p_rw.shape
    W = WIDTH
    has_vres = vres is not None
    row = lambda a: a.reshape(1, -1).astype(F32)
    full = lambda shape: pl.BlockSpec(shape, lambda b, i: (0,) * len(shape))
    tile = lambda c: pl.BlockSpec((1, tt, c), lambda b, i: (b, i, 0))
    blk = (jnp.arange(W)[:, None] // HEAD_DIM == jnp.arange(W)[None, :] // HEAD_DIM).astype(BF16)

    zpad = jnp.zeros((RW_LORA_PAD // 2, W), F32)
    wup = jnp.concatenate([prm['w_up'], zpad], axis=0).astype(BF16)
    aup = jnp.concatenate([zpad, prm['a_up']], axis=0).astype(BF16)
    ins = [p_rw, p_rw, row(prm['mu']), wup, row(prm['w0']), aup, row(prm['a0']),
           prm['g_up'].astype(BF16), row(prm['k_k']), row(prm['k_a']), row(prm['r_k']), blk]
    nprev = SUBLANES * 4 // p_rw.dtype.itemsize
    in_specs = [tile(cols),
                pl.BlockSpec((1, nprev, cols), lambda b, i: (b, jnp.maximum(i * (tt // nprev) - 1, 0), 0)),
                full((1, cols)), full((RW_LORA_PAD, W)), full((1, W)), full((RW_LORA_PAD, W)),
                full((1, W)), full((RW_LORA_PAD, W)), full((1, W)), full((1, W)), full((1, W)),
                full((W, W))]
    if has_vres:
        v_down, v_up, v0 = vres
        pad = LANES - RW_MV_LORA
        ins += [v_first, jnp.pad(v_down, ((0, 0), (0, pad))).astype(BF16),
                jnp.pad(v_up, ((0, pad), (0, 0))).astype(BF16), row(v0)]
        in_specs += [tile(W), full((W, LANES)), full((LANES, W)), full((1, W))]

    hm = jax.ShapeDtypeStruct((bsz, HEADS // 2, t, 2 * HEAD_DIM), F32)
    tw = jax.ShapeDtypeStruct((bsz, t, W), F32)
    hm_spec = pl.BlockSpec((1, HEADS // 2, tt, 2 * HEAD_DIM), lambda b, i: (b, 0, i, 0))
    out_shape = [hm, hm, hm, hm, tw, tw]
    out_specs = [hm_spec, hm_spec, hm_spec, hm_spec, tile(W), tile(W)]
    if not has_vres:
        out_shape.append(tw)
        out_specs.append(tile(W))
    outs = pl.pallas_call(
        functools.partial(_rwkv_a_kernel, tt=tt, has_vres=has_vres),
        grid=(bsz, t // tt),
        in_specs=in_specs, out_specs=out_specs, out_shape=out_shape,
        compiler_params=_cparams("parallel", "parallel"),
        name="rwkv_a",
    )(*ins)
    if has_vres:
        rq, y0, m, c, bonus, g = outs
    else:
        rq, y0, m, c, bonus, g, v_first = outs

    full1 = lambda shape: pl.BlockSpec(shape, lambda i: (0,) * len(shape))
    tile1 = pl.BlockSpec((bsz, tt, W), lambda i: (0, i, 0))
    hm1 = pl.BlockSpec((bsz, HEADS // 2, tt, 2 * HEAD_DIM), lambda i: (0, 0, i, 0))
    y = pl.pallas_call(
        functools.partial(_rwkv_b_kernel, tt=tt),
        grid=(t // tt,),
        in_specs=[hm1, hm1, hm1, hm1, tile1, tile1, full1((1, W)), full1((1, W)), full1((W, W))],
        out_specs=tile1,
        out_shape=jax.ShapeDtypeStruct((bsz, t, W), BF16),
        scratch_shapes=[pltpu.VMEM((bsz * HEADS // 2, 2 * HEAD_DIM, 2 * HEAD_DIM), F32),
                        pltpu.VMEM((bsz, tt, W), F32)],
        compiler_params=_cparams("arbitrary"),
        name="rwkv_b",
    )(rq, y0, m, c, bonus, g, row(prm['gn_w']), row(prm['gn_b']), blk)
    return y, v_first


def _ssm_kernel(p_ref, pprev_ref, cw_ref, cb_ref, dtb_ref, alog_ref, dsk_ref, nw_ref, exp_ref,
                tri_ref, o_ref, st_s):
    for b in range(p_ref.shape[0]):
        _ssm_one(b, p_ref, pprev_ref, cw_ref, cb_ref, dtb_ref, alog_ref, dsk_ref, nw_ref, exp_ref, tri_ref,
                 o_ref, st_s)


def _ssm_one(b, p_ref, pprev_ref, cw_ref, cb_ref, dtb_ref, alog_ref, dsk_ref, nw_ref, exp_ref, tri_ref,
             o_ref, st_s):
    Q = SSM_CHUNK
    W = WIDTH
    nb = SSM_GROUPS * SSM_STATE
    gw = W // SSM_GROUPS

    @pl.when(pl.program_id(0) == 0)
    def _():
        st_s[b] = jnp.zeros(st_s.shape[1:], F32)

    p = p_ref[b]
    z = p[:, 0:W]
    xbc_in = p[:, W:2 * W + 2 * nb]
    dt_in = p[:, 2 * W + 2 * nb:]
    prev = jnp.where(pl.program_id(0) == 0, 0.0, pprev_ref[b][:, W:2 * W + 2 * nb])
    row = lax.broadcasted_iota(jnp.int32, (Q, 1), 0)
    acc = cb_ref[...] + cw_ref[SSM_CONV - 1:SSM_CONV, :] * xbc_in
    for j in range(1, SSM_CONV):
        sh = pltpu.roll(xbc_in, j, axis=0)
        for i in range(j):
            sh = jnp.where(row == i, prev[SUBLANES - j + i:SUBLANES - j + i + 1, :], sh)
        acc = acc + cw_ref[SSM_CONV - 1 - j:SSM_CONV - j, :] * sh
    xbc = _silu(acc)
    xs = xbc[:, 0:W]
    bm = xbc[:, W:W + nb]
    cm = xbc[:, W + nb:]

    dt = _softplus(dt_in + dtb_ref[...])
    a = -jnp.exp(alog_ref[...]) * dt
    a_cs = _dot_exact_lhs(tri_ref[...], a)
    a_cs_t = a_cs.T
    dt_e = _dot_exact_rhs(dt, exp_ref[...])
    acs_e = _dot_exact_rhs(a_cs, exp_ref[...])
    last_e = acs_e[Q - 1:Q, :]
    xdt = xs * dt_e
    xdec = xdt * jnp.exp(last_e - acs_e)

    li = lax.broadcasted_iota(jnp.int32, (Q, Q), 0)
    si = lax.broadcasted_iota(jnp.int32, (Q, Q), 1)
    causal = li >= si
    hg = HEADS // SSM_GROUPS
    ys = []
    for g in range(SSM_GROUPS):
        bg = bm[:, g * SSM_STATE:(g + 1) * SSM_STATE]
        cg = cm[:, g * SSM_STATE:(g + 1) * SSM_STATE]
        cb = _dot_nt(cg, bg)
        for hh in range(hg):
            h = g * hg + hh
            seg = a_cs[:, h:h + 1] - a_cs_t[h:h + 1, :]
            lmat = jnp.where(causal, jnp.exp(jnp.where(causal, seg, 0.0)), 0.0)
            ys.append(_dot(cb * lmat, xdt[:, h * HEAD_DIM:(h + 1) * HEAD_DIM]))
        st = st_s[b, :, g * gw:(g + 1) * gw]
        y_off = _dot(cg, st) * jnp.exp(acs_e[:, g * gw:(g + 1) * gw])
        ys.append(y_off)
        st_s[b, :, g * gw:(g + 1) * gw] = (jnp.exp(last_e[:, g * gw:(g + 1) * gw]) * st
                                           + _dot_tn(bg, xdec[:, g * gw:(g + 1) * gw]))
    n5 = hg + 1
    y = jnp.concatenate(
        [jnp.concatenate(ys[g * n5:g * n5 + hg], axis=1) + ys[g * n5 + hg] for g in range(SSM_GROUPS)],
        axis=1)
    y = (y + dsk_ref[...] * xs) * _silu(z)
    outs = []
    for g in range(SSM_GROUPS):
        yg = y[:, g * gw:(g + 1) * gw]
        outs.append(yg * lax.rsqrt(jnp.mean(yg * yg, axis=-1, keepdims=True) + RMS_EPS))
    o_ref[b] = (jnp.concatenate(outs, axis=1) * nw_ref[...]).astype(o_ref.dtype)


def _ssm_mix(p_ssm, prm):
    bsz, t, cols = p_ssm.shape
    W = WIDTH
    Q = SSM_CHUNK
    cdim = W + 2 * SSM_GROUPS * SSM_STATE
    row = lambda a: a.reshape(1, -1).astype(F32)
    padrow = lambda a: jnp.pad(a.astype(F32), (0, LANES - a.shape[0])).reshape(1, LANES)
    full = lambda shape: pl.BlockSpec(shape, lambda i: (0,) * len(shape))
    expand = (jnp.arange(LANES)[:, None] == jnp.arange(W)[None, :] // HEAD_DIM).astype(BF16)
    tri = (jnp.arange(Q)[:, None] >= jnp.arange(Q)[None, :]).astype(BF16)
    return pl.pallas_call(
        _ssm_kernel,
        grid=(t // Q,),
        in_specs=[pl.BlockSpec((bsz, Q, cols), lambda i: (0, i, 0)),
                  pl.BlockSpec((bsz, SUBLANES, cols),
                               lambda i: (0, jnp.maximum(i * (Q // SUBLANES) - 1, 0), 0)),
                  full((SSM_CONV, cdim)), full((1, cdim)), full((1, LANES)), full((1, LANES)),
                  full((1, W)), full((1, W)), full((LANES, W)), full((Q, Q))],
        out_specs=pl.BlockSpec((bsz, Q, W), lambda i: (0, i, 0)),
        out_shape=jax.ShapeDtypeStruct((bsz, t, W), BF16),
        scratch_shapes=[pltpu.VMEM((bsz, SSM_STATE, W), F32)],
        compiler_params=_cparams("arbitrary"),
        name="ssm",
    )(p_ssm, p_ssm, prm['conv_w'].T.astype(F32), row(prm['conv_b']), padrow(prm['dt_bias']),
      padrow(prm['a_log']), row(jnp.repeat(prm['d'], HEAD_DIM)), row(prm['norm_w']), expand, tri)


def _ret_log_gamma(h):
    return math.log1p(-(2.0 ** (-5.0 - h)))


def _ret_kernel(p_ref, freq_ref, sgn_ref, blk_ref, o_ref, st_s, dec_s):
    Q = RET_CHUNK
    W = WIDTH
    half = HEAD_DIM // 2
    bsz = p_ref.shape[0]

    @pl.when(pl.program_id(0) == 0)
    def _():
        st_s[...] = jnp.zeros_like(st_s)
        li = lax.broadcasted_iota(jnp.int32, (Q, Q), 0)
        si = lax.broadcasted_iota(jnp.int32, (Q, Q), 1)
        rel = (li - si).astype(F32)
        for h in range(HEADS):
            dec_s[h] = jnp.where(li >= si, jnp.exp(jnp.where(li >= si, rel, 0.0) * _ret_log_gamma(h)), 0.0)

    idx = lax.broadcasted_iota(jnp.int32, (Q, 1), 0)
    pos = (pl.program_id(0) * Q + idx).astype(F32)
    ang = pos * freq_ref[...]
    cos = jnp.concatenate([jnp.cos(ang)] * (W // LANES), axis=1)
    sin = jnp.concatenate([jnp.sin(ang) * sgn_ref[...]] * (W // LANES), axis=1)
    lane = lax.broadcasted_iota(jnp.int32, (1, W), 1)
    first_half = (lane & (HEAD_DIM - 1)) < half

    def rot(x):
        partner = jnp.where(first_half, pltpu.roll(x, W - half, axis=1), pltpu.roll(x, half, axis=1))
        return x * cos + partner * sin

    def heads(x):
        return jnp.stack([x[:, h * HEAD_DIM:(h + 1) * HEAD_DIM] for h in range(HEADS)], axis=0)

    def per_head(fn):
        return jnp.concatenate([jnp.stack([fn(_ret_log_gamma(h)) for h in range(HEADS)], axis=0)] * bsz, axis=0)

    idf = idx.astype(F32)
    xi = per_head(lambda lg: jnp.exp((idf + 1.0) * lg))
    zeta = per_head(lambda lg: jnp.exp((Q - 1.0 - idf) * lg))
    cdec = per_head(lambda lg: jnp.full((1, 1), math.exp(Q * lg), F32))
    dec = jnp.concatenate([dec_s[...]] * bsz, axis=0)

    ps = [p_ref[b].astype(F32) for b in range(bsz)]
    q = jnp.concatenate([heads(rot(p[:, 0:W])) for p in ps], axis=0)
    k = jnp.concatenate([heads(rot(p[:, W:2 * W]) * (HEAD_DIM ** -0.5)) for p in ps], axis=0)
    v = jnp.concatenate([heads(p[:, 2 * W:3 * W]) for p in ps], axis=0)
    st = st_s[...]
    y = _bmm(_bmm_nt(q, k) * dec, v) + _bmm(q, st) * xi
    st_s[...] = cdec * st + _bmm_tn(k * zeta, v)
    for b in range(bsz):
        yb = jnp.concatenate([y[b * HEADS + h] for h in range(HEADS)], axis=1)
        ms = _head_sum(yb * yb, blk_ref) * (1.0 / HEAD_DIM)
        o_ref[b] = (yb * lax.rsqrt(ms + RMS_EPS) * _silu(ps[b][:, 3 * W:])).astype(o_ref.dtype)


def _ret_mix(p_ret):
    bsz, t, cols = p_ret.shape
    W = WIDTH
    Q = RET_CHUNK
    half = HEAD_DIM // 2
    full = lambda shape: pl.BlockSpec(shape, lambda i: (0,) * len(shape))
    inv_freq = ROPE_BASE ** (-jnp.arange(half, dtype=F32) / half)
    freq = jnp.tile(inv_freq, LANES // half).reshape(1, LANES)
    sgn = jnp.where((jnp.arange(LANES) % HEAD_DIM) < half, -1.0, 1.0).astype(F32).reshape(1, LANES)
    blk = (jnp.arange(W)[:, None] // HEAD_DIM == jnp.arange(W)[None, :] // HEAD_DIM).astype(BF16)
    return pl.pallas_call(
        _ret_kernel,
        grid=(t // Q,),
        in_specs=[pl.BlockSpec((bsz, Q, cols), lambda i: (0, i, 0)),
                  full((1, LANES)), full((1, LANES)), full((W, W))],
        out_specs=pl.BlockSpec((bsz, Q, W), lambda i: (0, i, 0)),
        out_shape=jax.ShapeDtypeStruct((bsz, t, W), BF16),
        scratch_shapes=[pltpu.VMEM((bsz * HEADS, HEAD_DIM, HEAD_DIM), F32), pltpu.VMEM((HEADS, Q, Q), F32)],
        compiler_params=_cparams("arbitrary"),
        name="retention",
    )(p_ret, freq, sgn, blk)


def _merge_kernel(x_ref, gl_ref, y0_ref, y1_ref, y2_ref, wb_ref, wo_ref, o_ref):
    d = x_ref.shape[1]
    merged = None
    for i, y_ref in enumerate((y0_ref, y1_ref, y2_ref)):
        gate = _sigmoid(gl_ref[:, i * d:(i + 1) * d].astype(F32))
        term = gate * jnp.dot(y_ref[...], wb_ref[i], preferred_element_type=F32)
        merged = term if merged is None else merged + term
    o_ref[...] = x_ref[...] + _dot(merged, wo_ref[...])


def _merge(x2, gate_logits, ys, w_branch, w_out, tm):
    n, d = x2.shape
    W = WIDTH
    rows = lambda c: pl.BlockSpec((tm, c), lambda i: (i, 0))
    return pl.pallas_call(
        _merge_kernel,
        grid=(n // tm,),
        in_specs=[rows(d), rows(3 * d), rows(W), rows(W), rows(W),
                  pl.BlockSpec((3, W, d), lambda i: (0, 0, 0)),
                  pl.BlockSpec((d, d), lambda i: (0, 0))],
        out_specs=rows(d),
        out_shape=jax.ShapeDtypeStruct((n, d), F32),
        compiler_params=_cparams("parallel"),
        name="merge_out",
    )(x2, gate_logits, *ys, w_branch.astype(BF16), w_out.astype(BF16))


def _ffn_kernel(x_ref, nw_ref, wg_ref, wu_ref, wd_ref, fw_ref, o_ref, h_s, acc_s, *, final_norm):
    j = pl.program_id(1)

    @pl.when(j == 0)
    def _():
        h_s[...] = _rms_rows(x_ref[...], nw_ref[...]).astype(BF16)
        acc_s[...] = jnp.zeros_like(acc_s)

    h = h_s[...]
    act = _silu(jnp.dot(h, wg_ref[...], preferred_element_type=F32)) * jnp.dot(
        h, wu_ref[...], preferred_element_type=F32)
    acc_s[...] += _dot(act, wd_ref[...])

    @pl.when(j == pl.num_programs(1) - 1)
    def _():
        y = x_ref[...] + acc_s[...]
        if final_norm:
            y = _rms_rows(y, fw_ref[...])
        o_ref[...] = y


def _ffn(x2, norm_w, wg, wu, wd, final_w, tm, tf):
    n, d = x2.shape
    f = wg.shape[1]
    final_norm = final_w is not None
    fw = (final_w if final_norm else jnp.ones((d,), F32)).reshape(1, d)
    return pl.pallas_call(
        functools.partial(_ffn_kernel, final_norm=final_norm),
        grid=(n // tm, f // tf),
        in_specs=[pl.BlockSpec((tm, d), lambda i, j: (i, 0)), pl.BlockSpec((1, d), lambda i, j: (0, 0)),
                  pl.BlockSpec((d, tf), lambda i, j: (0, j)), pl.BlockSpec((d, tf), lambda i, j: (0, j)),
                  pl.BlockSpec((tf, d), lambda i, j: (j, 0)), pl.BlockSpec((1, d), lambda i, j: (0, 0))],
        out_specs=pl.BlockSpec((tm, d), lambda i, j: (i, 0)),
        out_shape=jax.ShapeDtypeStruct((n, d), F32),
        scratch_shapes=[pltpu.VMEM((tm, d), BF16), pltpu.VMEM((tm, d), F32)],
        compiler_params=_cparams("parallel", "arbitrary"),
        name="ffn",
    )(x2, norm_w.reshape(1, d), wg, wu, wd, fw)


def _router_kernel(x_ref, nw_ref, rt_ref, h_ref, gate_ref, pos_ref, post_ref, cnt_ref, *, n_experts):
    tm = x_ref.shape[0]
    h = _rms_rows(x_ref[...], nw_ref[...])
    h_ref[...] = h.astype(BF16)
    r_hi, r_lo = rt_ref[0], rt_ref[1]
    h_hi, h_mid, h_lo = _split3(h)
    d = lambda a, b: jnp.dot(a, b, preferred_element_type=F32)
    logits = d(h_hi, r_hi) + d(h_mid, r_hi) + d(h_hi, r_lo) + d(h_lo, r_hi) + d(h_mid, r_lo)
    lane = lax.broadcasted_iota(jnp.int32, logits.shape, 1)
    neg = jnp.float32(-jnp.inf)
    logits = jnp.where(lane < n_experts, logits, neg)
    m1 = jnp.max(logits, axis=-1, keepdims=True)
    i1 = jnp.min(jnp.where(logits == m1, lane, LANES), axis=-1, keepdims=True)
    rest = jnp.where(lane == i1, neg, logits)
    m2 = jnp.max(rest, axis=-1, keepdims=True)
    i2 = jnp.min(jnp.where(rest == m2, lane, LANES), axis=-1, keepdims=True)
    e2 = jnp.exp(m2 - m1)
    w1 = 1.0 / (1.0 + e2)
    gate_ref[...] = jnp.where(lane == i1, w1, 0.0) + jnp.where(lane == i2, e2 * w1, 0.0)
    member = (lane == i1) | (lane == i2)
    m = jnp.where(member, 1.0, 0.0)
    row = lax.broadcasted_iota(jnp.int32, (tm, 1), 0)
    c = m
    step = 1
    while step < tm:
        c = c + jnp.where(row >= step, pltpu.roll(c, step, axis=0), 0.0)
        step *= 2
    posm = jnp.where(member, c - m, -1.0)
    pos_ref[...] = posm
    post_ref[0] = posm.T[0:SUBLANES, :]
    cnt_ref[0] = c[tm - 1:tm, :].astype(jnp.int32)


def _moe_gather_kernel(gi, ge, gr, gs, gf, h_ref, post_ref, xs_init_ref, xs_ref, *, br):
    w = pl.program_id(0)
    flags = gf[w]

    @pl.when((flags & 1) == 1)
    def _():
        rank = post_ref[0, pl.ds(ge[w], 1), :]
        slot = jnp.where(rank >= 0.0, rank + gs[w].astype(F32), -1.0)
        rows_i = lax.broadcasted_iota(jnp.int32, (br, 1), 0).astype(F32)
        sel = jnp.where(slot == rows_i, 1.0, 0.0).astype(BF16)
        rows = jnp.dot(sel, h_ref[...], preferred_element_type=F32).astype(BF16)

        @pl.when((flags & 2) == 2)
        def _():
            xs_ref[...] = rows

        @pl.when((flags & 2) == 0)
        def _():
            xs_ref[...] = (xs_ref[...].astype(F32) + rows.astype(F32)).astype(BF16)


def _moe_expert_kernel(eid, act, x_ref, wg_ref, wu_ref, wd_ref, o_ref, acc_s):
    r = pl.program_id(0)
    j = pl.program_id(1)
    last = j == pl.num_programs(1) - 1

    @pl.when(act[r] == 1)
    def _():
        x = x_ref[...]
        a = _silu(jnp.dot(x, wg_ref[0], preferred_element_type=F32)) * jnp.dot(
            x, wu_ref[0], preferred_element_type=F32)
        y = _dot(a, wd_ref[0])

        @pl.when(j == 0)
        def _():
            acc_s[...] = y

        @pl.when(j > 0)
        def _():
            acc_s[...] += y

        @pl.when(last)
        def _():
            o_ref[...] = acc_s[...].astype(o_ref.dtype)

    @pl.when((act[r] == 0) & last)
    def _():
        o_ref[...] = jnp.zeros_like(o_ref)


def _moe_combine_kernel(ci, ce, cr, cs, cf, x_ref, gate_ref, pos_ref, ys_ref, fw_ref, o_ref, *, br, final_norm):
    w = pl.program_id(0)
    flags = cf[w]
    tm = x_ref.shape[0]

    @pl.when((flags & 1) == 1)
    def _():
        @pl.when((flags & 2) == 2)
        def _():
            o_ref[...] = x_ref[...]

        lane = lax.broadcasted_iota(jnp.int32, (tm, LANES), 1)
        pick = lambda ref: jnp.sum(jnp.where(lane == ce[w], ref[...], 0.0), axis=-1, keepdims=True)
        rank = pick(pos_ref)
        slot = jnp.where(rank >= 0.0, rank + cs[w].astype(F32), -1.0)
        cols_i = lax.broadcasted_iota(jnp.int32, (1, br), 1).astype(F32)
        sel_t = jnp.where(slot == cols_i, 1.0, 0.0).astype(BF16)
        o_ref[...] += pick(gate_ref) * jnp.dot(sel_t, ys_ref[...], preferred_element_type=F32)

        if final_norm:
            @pl.when((flags & 4) == 4)
            def _():
                o_ref[...] = _rms_rows(o_ref[...], fw_ref[...])


def _moe_plan(cnt, tm, br, rm):
    nt, n_e = cnt.shape
    s_max = (-(-nt * tm * 2 // rm) + n_e) * rm
    tot = cnt.sum(0)
    ptot = (tot + rm - 1) // rm * rm
    seg_end = jnp.cumsum(ptot)
    off = (seg_end - ptot)[None, :] + jnp.cumsum(cnt, 0) - cnt
    first_blk = off // br
    nb = jnp.where(cnt > 0, (off + cnt - 1) // br - first_blk + 1, 0)
    w_max = s_max // br + nt * n_e
    w = jnp.arange(w_max, dtype=jnp.int32)

    def work_list(order):
        nb_o = nb.reshape(-1)[order]
        incl = jnp.cumsum(nb_o)
        total = incl[-1]
        w_eff = jnp.minimum(w, total - 1)
        p = jnp.minimum(jnp.sum(incl[None, :] <= w_eff[:, None], axis=1), nt * n_e - 1)
        pair = order[p]
        i_w, e_w = pair // n_e, pair % n_e
        r_w = first_blk.reshape(-1)[pair] + (w_eff - (incl - nb_o)[p])
        shift = off.reshape(-1)[pair] - r_w * br
        return (i_w, e_w, r_w, shift, w < total)

    ids = jnp.arange(nt * n_e, dtype=jnp.int32)
    gi, ge, gr, gs, gv = work_list(ids.reshape(nt, n_e).T.reshape(-1))
    g_first = gv & jnp.concatenate([jnp.ones((1,), bool), gr[1:] != gr[:-1]])
    g_flags = gv.astype(jnp.int32) + 2 * g_first.astype(jnp.int32)
    ci, ce, cr, cs, cv = work_list(ids)
    c_first = cv & jnp.concatenate([jnp.ones((1,), bool), ci[1:] != ci[:-1]])
    nxt_valid = jnp.concatenate([cv[1:], jnp.zeros((1,), bool)])
    c_last = cv & (jnp.concatenate([ci[1:] != ci[:-1], jnp.ones((1,), bool)]) | ~nxt_valid)
    c_flags = cv.astype(jnp.int32) + 2 * c_first.astype(jnp.int32) + 4 * c_last.astype(jnp.int32)
    r0 = jnp.arange(s_max // rm, dtype=jnp.int32) * rm
    eid = jnp.minimum(jnp.sum(seg_end[None, :] <= r0[:, None], axis=1), n_e - 1)
    act = (r0 < seg_end[-1]).astype(jnp.int32)
    i32 = lambda *xs: tuple(x.astype(jnp.int32) for x in xs)
    return s_max, i32(gi, ge, gr, gs, g_flags), i32(ci, ce, cr, cs, c_flags), i32(eid, act)


def _moe(x2, norm_w, wg, wu, wd, router, final_w, tm, tf, br, rm):
    n, d = x2.shape
    n_experts, _, f = wg.shape
    nt = n // tm
    final_norm = final_w is not None
    fw = (final_w if final_norm else jnp.ones((d,), F32)).reshape(1, d)
    rt = jnp.pad(router.astype(F32), ((0, 0), (0, LANES - n_experts)))
    r_hi = rt.astype(BF16)
    r_lo = (rt - r_hi.astype(F32)).astype(BF16)
    rows = lambda c: pl.BlockSpec((tm, c), lambda i: (i, 0))
    h, gates, pos, pos_t, cnt = pl.pallas_call(
        functools.partial(_router_kernel, n_experts=n_experts),
        grid=(nt,),
        in_specs=[rows(d), pl.BlockSpec((1, d), lambda i: (0, 0)),
                  pl.BlockSpec((2, d, LANES), lambda i: (0, 0, 0))],
        out_specs=[rows(d), rows(LANES), rows(LANES),
                   pl.BlockSpec((1, SUBLANES, tm), lambda i: (i, 0, 0)),
                   pl.BlockSpec((1, 1, LANES), lambda i: (i, 0, 0))],
        out_shape=[jax.ShapeDtypeStruct((n, d), BF16), jax.ShapeDtypeStruct((n, LANES), F32),
                   jax.ShapeDtypeStruct((n, LANES), F32), jax.ShapeDtypeStruct((nt, SUBLANES, tm), F32),
                   jax.ShapeDtypeStruct((nt, 1, LANES), jnp.int32)],
        compiler_params=_cparams("parallel"),
        name="moe_router",
    )(x2, norm_w.reshape(1, d), jnp.stack([r_hi, r_lo]))
    s_max, g_meta, c_meta, e_meta = _moe_plan(cnt[:, 0, :n_experts], tm, br, rm)
    n_work = g_meta[0].shape[0]

    xs = pl.pallas_call(
        functools.partial(_moe_gather_kernel, br=br),
        grid_spec=pltpu.PrefetchScalarGridSpec(
            num_scalar_prefetch=5, grid=(n_work,),
            in_specs=[pl.BlockSpec((tm, d), lambda w, gi, ge, gr, gs, gf: (gi[w], 0)),
                      pl.BlockSpec((1, SUBLANES, tm), lambda w, gi, ge, gr, gs, gf: (gi[w], 0, 0)),
                      pl.BlockSpec(memory_space=pl.ANY)],
            out_specs=pl.BlockSpec((br, d), lambda w, gi, ge, gr, gs, gf: (gr[w], 0))),
        out_shape=jax.ShapeDtypeStruct((s_max, d), BF16),
        input_output_aliases={7: 0},
        compiler_params=_cparams("arbitrary"),
        name="moe_gather",
    )(*g_meta, h, pos_t, jnp.zeros((s_max, d), BF16))

    nj = f // tf
    jj = lambda j, r, act: j * act[r] + (nj - 1) * (1 - act[r])
    ys = pl.pallas_call(
        _moe_expert_kernel,
        grid_spec=pltpu.PrefetchScalarGridSpec(
            num_scalar_prefetch=2, grid=(s_max // rm, nj),
            in_specs=[pl.BlockSpec((rm, d), lambda r, j, eid, act: (r, 0)),
                      pl.BlockSpec((1, d, tf), lambda r, j, eid, act: (eid[r], 0, jj(j, r, act))),
                      pl.BlockSpec((1, d, tf), lambda r, j, eid, act: (eid[r], 0, jj(j, r, act))),
                      pl.BlockSpec((1, tf, d), lambda r, j, eid, act: (eid[r], jj(j, r, act), 0))],
            out_specs=pl.BlockSpec((rm, d), lambda r, j, eid, act: (r, 0)),
            scratch_shapes=[pltpu.VMEM((rm, d), F32)]),
        out_shape=jax.ShapeDtypeStruct((s_max, d), BF16),
        compiler_params=_cparams("parallel", "arbitrary"),
        name="moe_experts",
    )(*e_meta, xs, wg, wu, wd)

    tile = lambda c: pl.BlockSpec((tm, c), lambda w, ci, ce, cr, cs, cf: (ci[w], 0))
    return pl.pallas_call(
        functools.partial(_moe_combine_kernel, br=br, final_norm=final_norm),
        grid_spec=pltpu.PrefetchScalarGridSpec(
            num_scalar_prefetch=5, grid=(n_work,),
            in_specs=[tile(d), tile(LANES), tile(LANES),
                      pl.BlockSpec((br, d), lambda w, ci, ce, cr, cs, cf: (cr[w], 0)),
                      pl.BlockSpec((1, d), lambda w, ci, ce, cr, cs, cf: (0, 0))],
            out_specs=tile(d)),
        out_shape=jax.ShapeDtypeStruct((n, d), F32),
        compiler_params=_cparams("arbitrary"),
        name="moe_combine",
    )(*c_meta, x2, gates, pos, ys, fw)


def _tiles(n_rows, t):
    tm = 512 if n_rows % 512 == 0 else 256
    tt = 256 if t % 256 == 0 else RW_CHUNK
    tm_moe = 1024 if n_rows % 1024 == 0 else tm
    br, rm = 512, 512
    return dict(tm=tm, tt=tt, tm_moe=tm_moe, br=br, rm=rm)


def _ff_tile(f, max_tile=2048):
    for parts in (2, 4, 7, 11, 14, 22, 28):
        if f % parts == 0 and (f // parts) % LANES == 0 and f // parts <= max_tile:
            return f // parts
    return f


def _deinterleave_heads(w):
    d_in = w.shape[0]
    return w.reshape(d_in, HEADS, HEAD_DIM // 2, 2).transpose(0, 1, 3, 2).reshape(d_in, WIDTH)


def kernel(x, norm1_w, w_in, rw_mu, rw_w_up, rw_w0, rw_a_up, rw_a0, rw_g_up, rw_k_k, rw_k_a, rw_r_k, rw_gn_w, rw_gn_b, rw_v_down, rw_v_up, rw_v0, ssm_conv_w, ssm_conv_b, ssm_dt_bias, ssm_a_log, ssm_d, ssm_norm_w, w_branch, w_out, norm2_w, ffn_wg, ffn_wu, ffn_wd, moe_router, moe_wg, moe_wu, moe_wd, final_norm_w):
    bsz, t, d = x.shape
    depth = w_in.shape[0]
    n = bsz * t
    W = WIDTH
    tl = _tiles(n, t)
    rw_cols = 3 * W + rw_w_up.shape[1] + rw_a_up.shape[1] + rw_g_up.shape[1]
    cdim = ssm_conv_w.shape[1]
    ssm_cols = W + cdim + ssm_a_log.shape[1]
    c1 = rw_cols
    c2 = c1 + ssm_cols
    c3 = c2 + 4 * W
    ssm_pad = (-ssm_cols) % LANES

    x2 = x.reshape(n, d)
    v_first = None
    for layer in range(depth):
        wl = w_in[layer]
        w_rw = wl[:, :c1].astype(BF16)
        w_ssm = jnp.pad(wl[:, c1:c2], ((0, 0), (0, ssm_pad))).astype(BF16)
        w_ret = jnp.concatenate([_deinterleave_heads(wl[:, c2:c2 + W]),
                                 _deinterleave_heads(wl[:, c2 + W:c2 + 2 * W]),
                                 wl[:, c2 + 2 * W:c3]], axis=1).astype(BF16)
        w_gate = wl[:, c3:].astype(BF16)
        p_rw = _norm_proj(x2, norm1_w[layer], w_rw, tl['tm'], BF16).reshape(bsz, t, -1)
        p_ssm = _norm_proj(x2, norm1_w[layer], w_ssm, tl['tm']).reshape(bsz, t, -1)
        p_ret = _norm_proj(x2, norm1_w[layer], w_ret, tl['tm'], BF16).reshape(bsz, t, -1)
        gate_logits = _norm_proj(x2, norm1_w[layer], w_gate, tl['tm'], BF16)

        rw_prm = dict(mu=rw_mu[layer], w_up=rw_w_up[layer], w0=rw_w0[layer], a_up=rw_a_up[layer],
                      a0=rw_a0[layer], g_up=rw_g_up[layer], k_k=rw_k_k[layer], k_a=rw_k_a[layer],
                      r_k=rw_r_k[layer].reshape(-1), gn_w=rw_gn_w[layer], gn_b=rw_gn_b[layer])
        vres = None if layer == 0 else (rw_v_down[layer - 1], rw_v_up[layer - 1], rw_v0[layer - 1])
        y_rw, v_first = _rwkv_mix(p_rw, rw_prm, v_first, vres, tl['tt'])
        ssm_prm = dict(conv_w=ssm_conv_w[layer], conv_b=ssm_conv_b[layer], dt_bias=ssm_dt_bias[layer],
                       a_log=ssm_a_log[layer], d=ssm_d[layer], norm_w=ssm_norm_w[layer])
        y_ssm = _ssm_mix(p_ssm, ssm_prm)
        y_ret = _ret_mix(p_ret)
        ys = [y.reshape(n, W) for y in (y_rw, y_ssm, y_ret)]
        x2 = _merge(x2, gate_logits, ys, w_branch[layer], w_out[layer], tl['tm'])

        j = layer // 2
        final_w = final_norm_w if layer == depth - 1 else None
        if layer % 2 == 0:
            x2 = _ffn(x2, norm2_w[layer], ffn_wg[j].astype(BF16), ffn_wu[j].astype(BF16),
                      ffn_wd[j].astype(BF16), final_w, tl['tm_moe'], _ff_tile(ffn_wg.shape[2]))
        else:
            x2 = _moe(x2, norm2_w[layer], moe_wg[j].astype(BF16), moe_wu[j].astype(BF16),
                      moe_wd[j].astype(BF16), moe_router[j], final_w, tl['tm_moe'],
                      _ff_tile(moe_wg.shape[3], 1024), tl['br'], tl['rm'])
    return x2.reshape(bsz, t, d)
```

```python
import functools
import math

import jax
import jax.numpy as jnp
from jax import lax
from jax.experimental import pallas as pl
from jax.experimental.pallas import tpu as pltpu

F32 = jnp.float32
BF16 = jnp.bfloat16

LANES = 128
SUBLANES = 8
VMEM_LIMIT_BYTES = 56 * 1024 * 1024

RMS_EPS = 1e-6
RW_GN_EPS = 64e-5
HEADS = 8
HEAD_DIM = 64
WIDTH = HEADS * HEAD_DIM
RW_LORA_PAD = 128
RW_MV_LORA = 32
SSM_GROUPS = 2
SSM_STATE = 128
SSM_CONV = 4
ROPE_BASE = 10000.0
N_EXPERTS = 8

RW_CHUNK = 64
SSM_CHUNK = 128
RET_CHUNK = 128


def _cparams(*sem):
    return pltpu.CompilerParams(dimension_semantics=sem, vmem_limit_bytes=VMEM_LIMIT_BYTES)


def _dot(a, b):
    return jnp.dot(a.astype(BF16), b.astype(BF16), preferred_element_type=F32)


def _dot_nt(a, b):
    return lax.dot_general(a.astype(BF16), b.astype(BF16), (((1,), (1,)), ((), ())),
                           preferred_element_type=F32)


def _dot_tn(a, b):
    return lax.dot_general(a.astype(BF16), b.astype(BF16), (((0,), (0,)), ((), ())),
                           preferred_element_type=F32)


def _bmm(a, b):
    return lax.dot_general(a.astype(BF16), b.astype(BF16), (((2,), (1,)), ((0,), (0,))),
                           preferred_element_type=F32)


def _bmm_nt(a, b):
    return lax.dot_general(a.astype(BF16), b.astype(BF16), (((2,), (2,)), ((0,), (0,))),
                           preferred_element_type=F32)


def _bmm_tn(a, b):
    return lax.dot_general(a.astype(BF16), b.astype(BF16), (((1,), (1,)), ((0,), (0,))),
                           preferred_element_type=F32)


def _split2(x):
    hi = x.astype(BF16)
    lo = (x - hi.astype(F32)).astype(BF16)
    return hi, lo


def _split3(x):
    hi = x.astype(BF16)
    r = x - hi.astype(F32)
    mid = r.astype(BF16)
    lo = (r - mid.astype(F32)).astype(BF16)
    return hi, mid, lo


def _dot_exact_rhs(x, m_bf16):
    hi, mid, lo = _split3(x)
    d = lambda a: jnp.dot(a, m_bf16, preferred_element_type=F32)
    return d(hi) + d(mid) + d(lo)


def _dot_exact_lhs(m_bf16, x):
    hi, mid, lo = _split3(x)
    d = lambda a: jnp.dot(m_bf16, a, preferred_element_type=F32)
    return d(hi) + d(mid) + d(lo)


def _sigmoid(x):
    return 1.0 / (1.0 + jnp.exp(-x))


def _silu(x):
    return x * _sigmoid(x)


def _softplus(x):
    return jnp.maximum(x, 0.0) + jnp.log(1.0 + jnp.exp(-jnp.abs(x)))


def _rms_rows(x, w):
    return x * lax.rsqrt(jnp.mean(x * x, axis=-1, keepdims=True) + RMS_EPS) * w


def _norm_proj_kernel(x_ref, nw_ref, w_ref, o_ref):
    h = _rms_rows(x_ref[...], nw_ref[...]).astype(BF16)
    o_ref[...] = jnp.dot(h, w_ref[...], preferred_element_type=F32).astype(o_ref.dtype)


def _norm_proj(x2, norm_w, w_bf16, tm, out_dtype=F32):
    n, d = x2.shape
    c = w_bf16.shape[1]
    return pl.pallas_call(
        _norm_proj_kernel,
        grid=(n // tm,),
        in_specs=[pl.BlockSpec((tm, d), lambda i: (i, 0)),
                  pl.BlockSpec((1, d), lambda i: (0, 0)),
                  pl.BlockSpec((d, c), lambda i: (0, 0))],
        out_specs=pl.BlockSpec((tm, c), lambda i: (i, 0)),
        out_shape=jax.ShapeDtypeStruct((n, c), out_dtype),
        compiler_params=_cparams("parallel"),
        name="norm_proj",
    )(x2, norm_w.reshape(1, d), w_bf16)


def _head_sum(x, blk_ref):
    hi, lo = _split2(x)
    m = blk_ref[...]
    return jnp.dot(hi, m, preferred_element_type=F32) + jnp.dot(lo, m, preferred_element_type=F32)


def _rwkv_a_kernel(*refs, tt, has_vres):
    if has_vres:
        (p_ref, pprev_ref, mu_ref, wup_ref, w0_ref, aup_ref, a0_ref, gup_ref, kk_ref, ka_ref,
         rk_ref, blk_ref, vfirst_ref, vdown_ref, vup_ref, v0_ref,
         rq_ref, y0_ref, m_ref, c_ref, bonus_ref, g_ref) = refs
        vout_ref = None
    else:
        (p_ref, pprev_ref, mu_ref, wup_ref, w0_ref, aup_ref, a0_ref, gup_ref, kk_ref, ka_ref,
         rk_ref, blk_ref,
         rq_ref, y0_ref, m_ref, c_ref, bonus_ref, g_ref, vout_ref) = refs
    L = RW_CHUNK
    nchunk = tt // L
    W = WIDTH

    p = p_ref[0].astype(F32)
    nprev = pprev_ref.shape[1]
    prev = jnp.where(pl.program_id(1) == 0, 0.0, pprev_ref[0][nprev - 1:nprev, :].astype(F32))
    row = lax.broadcasted_iota(jnp.int32, (tt, 1), 0)
    shifted = jnp.where(row == 0, prev, pltpu.roll(p, 1, axis=0))
    xm = p + (shifted - p) * mu_ref[...]

    r = xm[:, 0:W]
    k = xm[:, W:2 * W]
    v = xm[:, 2 * W:3 * W]
    xwa = xm[:, 3 * W:3 * W + RW_LORA_PAD]
    xg = xm[:, 3 * W + RW_LORA_PAD:]

    ld = -math.exp(-0.5) * _sigmoid(w0_ref[...] + _dot(jnp.tanh(xwa), wup_ref[...]))
    a_sig = _sigmoid(a0_ref[...] + _dot(xwa, aup_ref[...]))
    g_ref[0] = _dot(_sigmoid(xg), gup_ref[...])
    if has_vres:
        vu = _dot(_dot(v, vdown_ref[...]), vup_ref[...])
        v = v + (vfirst_ref[0] - v) * _sigmoid(v0_ref[...] + vu)
    else:
        vout_ref[0] = v

    kk = k * kk_ref[...]
    kk = kk * lax.rsqrt(jnp.maximum(_head_sum(kk * kk, blk_ref), 1e-24))
    k = k * (1.0 + (a_sig - 1.0) * ka_ref[...])
    bonus_ref[0] = _head_sum(r * k * rk_ref[...], blk_ref) * v
    avec = -kk
    bvec = kk * a_sig

    cum = ld
    pos = row & (L - 1)
    d = 1
    while d < L:
        cum = cum + jnp.where(pos >= d, pltpu.roll(cum, d, axis=0), 0.0)
        d *= 2
    cum3 = cum.reshape(nchunk, L, W)
    last = jnp.broadcast_to(cum3[:, L - 1:L, :], (nchunk, L, W)).reshape(tt, W)
    w_inv = jnp.exp(-cum)
    w_last = jnp.exp(last - cum)

    def to_g(val):
        val = val.astype(BF16)
        return jnp.concatenate(
            [val[:, h * HEAD_DIM:(h + 1) * HEAD_DIM].reshape(nchunk, L, HEAD_DIM) for h in range(HEADS)],
            axis=0)

    at = to_g(avec * jnp.exp(cum - ld))
    rt = to_g(r * jnp.exp(cum))
    bt = to_g(bvec * w_inv)
    kt = to_g(k * w_inv)
    bh = to_g(bvec * w_last)
    kh = to_g(k * w_last)
    vv = to_g(v)
    wl = jnp.concatenate(
        [jnp.exp(cum3[:, L - 1:L, h * HEAD_DIM:(h + 1) * HEAD_DIM]) for h in range(HEADS)], axis=0)

    ri = lax.broadcasted_iota(jnp.int32, (1, L, L), 1)
    ci = lax.broadcasted_iota(jnp.int32, (1, L, L), 2)
    strict = ri > ci
    incl = ri >= ci
    eye = ri == ci

    g4 = _bmm_nt(jnp.concatenate([at, rt], axis=1), jnp.concatenate([bt, kt], axis=1))
    nmat = jnp.where(strict, g4[:, 0:L, 0:L], 0.0)
    a_ak = jnp.where(strict, g4[:, 0:L, L:2 * L], 0.0)
    a_rb = jnp.where(incl, g4[:, L:2 * L, 0:L], 0.0)
    a_rk = jnp.where(incl, g4[:, L:2 * L, L:2 * L], 0.0)
    tp = nmat
    npow = nmat
    span = 1
    while span * 2 < L:
        npow = _bmm(npow, npow)
        tp = tp + npow + _bmm(tp, npow)
        span *= 2
    av = _bmm(jnp.concatenate([a_ak, a_rk], axis=1), vv)
    x0 = jnp.concatenate([at.astype(F32), av[:, 0:L]], axis=2)
    x = x0 + _bmm(tp, x0)
    ry = jnp.concatenate([rt.astype(F32), av[:, L:2 * L]], axis=2) + _bmm(a_rb, x)
    bx = _bmm_tn(bh, x)
    kv = _bmm_tn(kh, vv)
    mc = bx + jnp.concatenate([jnp.where(eye, wl, 0.0), kv], axis=2)

    def pairs(z, out_lo, out_hi):
        z = z.reshape(HEADS // 2, 2, tt, 2 * HEAD_DIM)
        za, zb = z[:, 0], z[:, 1]
        low = lax.broadcasted_iota(jnp.int32, (1, 1, 2 * HEAD_DIM), 2) < HEAD_DIM
        out_lo[0] = jnp.where(low, za, pltpu.roll(zb, HEAD_DIM, axis=2))
        out_hi[0] = jnp.where(low, pltpu.roll(za, HEAD_DIM, axis=2), zb)

    pairs(ry, rq_ref, y0_ref)
    pairs(mc, m_ref, c_ref)


def _rwkv_b_kernel(rq_ref, y0_ref, m_ref, c_ref, bonus_ref, g_ref, gnw_ref, gnb_ref, blk_ref, o_ref,
                   st_s, y_s, *, tt):
    L = RW_CHUNK
    nchunk = tt // L
    bsz = rq_ref.shape[0]
    npair = HEADS // 2
    P = 2 * HEAD_DIM

    @pl.when(pl.program_id(0) == 0)
    def _():
        st_s[...] = jnp.zeros_like(st_s)

    ri = lax.broadcasted_iota(jnp.int32, (P, P), 0) < HEAD_DIM
    ci = lax.broadcasted_iota(jnp.int32, (P, P), 1) < HEAD_DIM
    diag_blocks = ri == ci

    def block_diag(z):
        return jnp.where(diag_blocks, jnp.concatenate([z, z], axis=0), 0.0)

    def chunk_body(c, carry):
        r0 = pl.multiple_of(c * L, L)
        d = lambda a, b_: jnp.dot(a, b_, preferred_element_type=F32)
        for b in range(bsz):
            for p in range(npair):
                st = st_s[b * npair + p]
                rows = (b, p, pl.ds(r0, L), slice(None))
                y = y0_ref[rows] + _dot(rq_ref[rows], st)
                m_hi, m_lo = _split2(block_diag(m_ref[rows]))
                s_hi, s_lo = _split2(st)
                st_s[b * npair + p] = block_diag(c_ref[rows]) + d(m_hi, s_hi) + d(m_lo, s_hi) + d(m_hi, s_lo)
                y_s[b, pl.ds(r0, L), p * P:(p + 1) * P] = y
        return carry

    lax.fori_loop(0, nchunk, chunk_body, 0)

    inv = 1.0 / HEAD_DIM
    for b in range(bsz):
        y = y_s[b]
        mean = _head_sum(y, blk_ref) * inv
        dlt = y - mean
        var = _head_sum(dlt * dlt, blk_ref) * inv
        yn = dlt * lax.rsqrt(var + RW_GN_EPS) * gnw_ref[...] + gnb_ref[...]
        o_ref[b] = ((yn + bonus_ref[b]) * g_ref[b]).astype(o_ref.dtype)


def _rwkv_mix(p_rw, prm, v_first, vres, tt):
    bsz, t, cols = p_rw.shape
    W = WIDTH
    has_vres = vres is not None
    row = lambda a: a.reshape(1, -1).astype(F32)
    full = lambda shape: pl.BlockSpec(shape, lambda b, i: (0,) * len(shape))
    tile = lambda c: pl.BlockSpec((1, tt, c), lambda b, i: (b, i, 0))
    blk = (jnp.arange(W)[:, None] // HEAD_DIM == jnp.arange(W)[None, :] // HEAD_DIM).astype(BF16)

    zpad = jnp.zeros((RW_LORA_PAD // 2, W), F32)
    wup = jnp.concatenate([prm['w_up'], zpad], axis=0).astype(BF16)
    aup = jnp.concatenate([zpad, prm['a_up']], axis=0).astype(BF16)
    ins = [p_rw, p_rw, row(prm['mu']), wup, row(prm['w0']), aup, row(prm['a0']),
           prm['g_up'].astype(BF16), row(prm['k_k']), row(prm['k_a']), row(prm['r_k']), blk]
    nprev = SUBLANES * 4 // p_rw.dtype.itemsize
    in_specs = [tile(cols),
                pl.BlockSpec((1, nprev, cols), lambda b, i: (b, jnp.maximum(i * (tt // nprev) - 1, 0), 0)),
                full((1, cols)), full((RW_LORA_PAD, W)), full((1, W)), full((RW_LORA_PAD, W)),
                full((1, W)), full((RW_LORA_PAD, W)), full((1, W)), full((1, W)), full((1, W)),
                full((W, W))]
    if has_vres:
        v_down, v_up, v0 = vres
        pad = LANES - RW_MV_LORA
        ins += [v_first, jnp.pad(v_down, ((0, 0), (0, pad))).astype(BF16),
                jnp.pad(v_up, ((0, pad), (0, 0))).astype(BF16), row(v0)]
        in_specs += [tile(W), full((W, LANES)), full((LANES, W)), full((1, W))]

    hm = jax.ShapeDtypeStruct((bsz, HEADS // 2, t, 2 * HEAD_DIM), F32)
    tw = jax.ShapeDtypeStruct((bsz, t, W), F32)
    hm_spec = pl.BlockSpec((1, HEADS // 2, tt, 2 * HEAD_DIM), lambda b, i: (b, 0, i, 0))
    out_shape = [hm, hm, hm, hm, tw, tw]
    out_specs = [hm_spec, hm_spec, hm_spec, hm_spec, tile(W), tile(W)]
    if not has_vres:
        out_shape.append(tw)
        out_specs.append(tile(W))
    outs = pl.pallas_call(
        functools.partial(_rwkv_a_kernel, tt=tt, has_vres=has_vres),
        grid=(bsz, t // tt),
        in_specs=in_specs, out_specs=out_specs, out_shape=out_shape,
        compiler_params=_cparams("parallel", "parallel"),
        name="rwkv_a",
    )(*ins)
    if has_vres:
        rq, y0, m, c, bonus, g = outs
    else:
        rq, y0, m, c, bonus, g, v_first = outs

    full1 = lambda shape: pl.BlockSpec(shape, lambda i: (0,) * len(shape))
    tile1 = pl.BlockSpec((bsz, tt, W), lambda i: (0, i, 0))
    hm1 = pl.BlockSpec((bsz, HEADS // 2, tt, 2 * HEAD_DIM), lambda i: (0, 0, i, 0))
    y = pl.pallas_call(
        functools.partial(_rwkv_b_kernel, tt=tt),
        grid=(t // tt,),
        in_specs=[hm1, hm1, hm1, hm1, tile1, tile1, full1((1, W)), full1((1, W)), full1((W, W))],
        out_specs=tile1,
        out_shape=jax.ShapeDtypeStruct((bsz, t, W), BF16),
        scratch_shapes=[pltpu.VMEM((bsz * HEADS // 2, 2 * HEAD_DIM, 2 * HEAD_DIM), F32),
                        pltpu.VMEM((bsz, tt, W), F32)],
        compiler_params=_cparams("arbitrary"),
        name="rwkv_b",
    )(rq, y0, m, c, bonus, g, row(prm['gn_w']), row(prm['gn_b']), blk)
    return y, v_first


def _ssm_kernel(p_ref, pprev_ref, cw_ref, cb_ref, dtb_ref, alog_ref, dsk_ref, nw_ref, exp_ref,
                tri_ref, o_ref, st_s):
    for b in range(p_ref.shape[0]):
        _ssm_one(b, p_ref, pprev_ref, cw_ref, cb_ref, dtb_ref, alog_ref, dsk_ref, nw_ref, exp_ref, tri_ref,
                 o_ref, st_s)


def _ssm_one(b, p_ref, pprev_ref, cw_ref, cb_ref, dtb_ref, alog_ref, dsk_ref, nw_ref, exp_ref, tri_ref,
             o_ref, st_s):
    Q = SSM_CHUNK
    W = WIDTH
    nb = SSM_GROUPS * SSM_STATE
    gw = W // SSM_GROUPS

    @pl.when(pl.program_id(0) == 0)
    def _():
        st_s[b] = jnp.zeros(st_s.shape[1:], F32)

    p = p_ref[b]
    z = p[:, 0:W]
    xbc_in = p[:, W:2 * W + 2 * nb]
    dt_in = p[:, 2 * W + 2 * nb:]
    prev = jnp.where(pl.program_id(0) == 0, 0.0, pprev_ref[b][:, W:2 * W + 2 * nb])
    row = lax.broadcasted_iota(jnp.int32, (Q, 1), 0)
    acc = cb_ref[...] + cw_ref[SSM_CONV - 1:SSM_CONV, :] * xbc_in
    for j in range(1, SSM_CONV):
        sh = pltpu.roll(xbc_in, j, axis=0)
        for i in range(j):
            sh = jnp.where(row == i, prev[SUBLANES - j + i:SUBLANES - j + i + 1, :], sh)
        acc = acc + cw_ref[SSM_CONV - 1 - j:SSM_CONV - j, :] * sh
    xbc = _silu(acc)
    xs = xbc[:, 0:W]
    bm = xbc[:, W:W + nb]
    cm = xbc[:, W + nb:]

    dt = _softplus(dt_in + dtb_ref[...])
    a = -jnp.exp(alog_ref[...]) * dt
    a_cs = _dot_exact_lhs(tri_ref[...], a)
    a_cs_t = a_cs.T
    dt_e = _dot_exact_rhs(dt, exp_ref[...])
    acs_e = _dot_exact_rhs(a_cs, exp_ref[...])
    last_e = acs_e[Q - 1:Q, :]
    xdt = xs * dt_e
    xdec = xdt * jnp.exp(last_e - acs_e)

    li = lax.broadcasted_iota(jnp.int32, (Q, Q), 0)
    si = lax.broadcasted_iota(jnp.int32, (Q, Q), 1)
    causal = li >= si
    hg = HEADS // SSM_GROUPS
    ys = []
    for g in range(SSM_GROUPS):
        bg = bm[:, g * SSM_STATE:(g + 1) * SSM_STATE]
        cg = cm[:, g * SSM_STATE:(g + 1) * SSM_STATE]
        cb = _dot_nt(cg, bg)
        for hh in range(hg):
            h = g * hg + hh
            seg = a_cs[:, h:h + 1] - a_cs_t[h:h + 1, :]
            lmat = jnp.where(causal, jnp.exp(jnp.where(causal, seg, 0.0)), 0.0)
            ys.append(_dot(cb * lmat, xdt[:, h * HEAD_DIM:(h + 1) * HEAD_DIM]))
        st = st_s[b, :, g * gw:(g + 1) * gw]
        y_off = _dot(cg, st) * jnp.exp(acs_e[:, g * gw:(g + 1) * gw])
        ys.append(y_off)
        st_s[b, :, g * gw:(g + 1) * gw] = (jnp.exp(last_e[:, g * gw:(g + 1) * gw]) * st
                                           + _dot_tn(bg, xdec[:, g * gw:(g + 1) * gw]))
    n5 = hg + 1
    y = jnp.concatenate(
        [jnp.concatenate(ys[g * n5:g * n5 + hg], axis=1) + ys[g * n5 + hg] for g in range(SSM_GROUPS)],
        axis=1)
    y = (y + dsk_ref[...] * xs) * _silu(z)
    outs = []
    for g in range(SSM_GROUPS):
        yg = y[:, g * gw:(g + 1) * gw]
        outs.append(yg * lax.rsqrt(jnp.mean(yg * yg, axis=-1, keepdims=True) + RMS_EPS))
    o_ref[b] = (jnp.concatenate(outs, axis=1) * nw_ref[...]).astype(o_ref.dtype)


def _ssm_mix(p_ssm, prm):
    bsz, t, cols = p_ssm.shape
    W = WIDTH
    Q = SSM_CHUNK
    cdim = W + 2 * SSM_GROUPS * SSM_STATE
    row = lambda a: a.reshape(1, -1).astype(F32)
    padrow = lambda a: jnp.pad(a.astype(F32), (0, LANES - a.shape[0])).reshape(1, LANES)
    full = lambda shape: pl.BlockSpec(shape, lambda i: (0,) * len(shape))
    expand = (jnp.arange(LANES)[:, None] == jnp.arange(W)[None, :] // HEAD_DIM).astype(BF16)
    tri = (jnp.arange(Q)[:, None] >= jnp.arange(Q)[None, :]).astype(BF16)
    return pl.pallas_call(
        _ssm_kernel,
        grid=(t // Q,),
        in_specs=[pl.BlockSpec((bsz, Q, cols), lambda i: (0, i, 0)),
                  pl.BlockSpec((bsz, SUBLANES, cols),
                               lambda i: (0, jnp.maximum(i * (Q // SUBLANES) - 1, 0), 0)),
                  full((SSM_CONV, cdim)), full((1, cdim)), full((1, LANES)), full((1, LANES)),
                  full((1, W)), full((1, W)), full((LANES, W)), full((Q, Q))],
        out_specs=pl.BlockSpec((bsz, Q, W), lambda i: (0, i, 0)),
        out_shape=jax.ShapeDtypeStruct((bsz, t, W), BF16),
        scratch_shapes=[pltpu.VMEM((bsz, SSM_STATE, W), F32)],
        compiler_params=_cparams("arbitrary"),
        name="ssm",
    )(p_ssm, p_ssm, prm['conv_w'].T.astype(F32), row(prm['conv_b']), padrow(prm['dt_bias']),
      padrow(prm['a_log']), row(jnp.repeat(prm['d'], HEAD_DIM)), row(prm['norm_w']), expand, tri)


def _ret_log_gamma(h):
    return math.log1p(-(2.0 ** (-5.0 - h)))


def _ret_kernel(p_ref, freq_ref, sgn_ref, blk_ref, o_ref, st_s, dec_s):
    Q = RET_CHUNK
    W = WIDTH
    half = HEAD_DIM // 2
    bsz = p_ref.shape[0]

    @pl.when(pl.program_id(0) == 0)
    def _():
        st_s[...] = jnp.zeros_like(st_s)
        li = lax.broadcasted_iota(jnp.int32, (Q, Q), 0)
        si = lax.broadcasted_iota(jnp.int32, (Q, Q), 1)
        rel = (li - si).astype(F32)
        for h in range(HEADS):
            dec_s[h] = jnp.where(li >= si, jnp.exp(jnp.where(li >= si, rel, 0.0) * _ret_log_gamma(h)), 0.0)

    idx = lax.broadcasted_iota(jnp.int32, (Q, 1), 0)
    pos = (pl.program_id(0) * Q + idx).astype(F32)
    ang = pos * freq_ref[...]
    cos = jnp.concatenate([jnp.cos(ang)] * (W // LANES), axis=1)
    sin = jnp.concatenate([jnp.sin(ang) * sgn_ref[...]] * (W // LANES), axis=1)
    lane = lax.broadcasted_iota(jnp.int32, (1, W), 1)
    first_half = (lane & (HEAD_DIM - 1)) < half

    def rot(x):
        partner = jnp.where(first_half, pltpu.roll(x, W - half, axis=1), pltpu.roll(x, half, axis=1))
        return x * cos + partner * sin

    def heads(x):
        return jnp.stack([x[:, h * HEAD_DIM:(h + 1) * HEAD_DIM] for h in range(HEADS)], axis=0)

    def per_head(fn):
        return jnp.concatenate([jnp.stack([fn(_ret_log_gamma(h)) for h in range(HEADS)], axis=0)] * bsz, axis=0)

    idf = idx.astype(F32)
    xi = per_head(lambda lg: jnp.exp((idf + 1.0) * lg))
    zeta = per_head(lambda lg: jnp.exp((Q - 1.0 - idf) * lg))
    cdec = per_head(lambda lg: jnp.full((1, 1), math.exp(Q * lg), F32))
    dec = jnp.concatenate([dec_s[...]] * bsz, axis=0)

    ps = [p_ref[b].astype(F32) for b in range(bsz)]
    q = jnp.concatenate([heads(rot(p[:, 0:W])) for p in ps], axis=0)
    k = jnp.concatenate([heads(rot(p[:, W:2 * W]) * (HEAD_DIM ** -0.5)) for p in ps], axis=0)
    v = jnp.concatenate([heads(p[:, 2 * W:3 * W]) for p in ps], axis=0)
    st = st_s[...]
    y = _bmm(_bmm_nt(q, k) * dec, v) + _bmm(q, st) * xi
    st_s[...] = cdec * st + _bmm_tn(k * zeta, v)
    for b in range(bsz):
        yb = jnp.concatenate([y[b * HEADS + h] for h in range(HEADS)], axis=1)
        ms = _head_sum(yb * yb, blk_ref) * (1.0 / HEAD_DIM)
        o_ref[b] = (yb * lax.rsqrt(ms + RMS_EPS) * _silu(ps[b][:, 3 * W:])).astype(o_ref.dtype)


def _ret_mix(p_ret):
    bsz, t, cols = p_ret.shape
    W = WIDTH
    Q = RET_CHUNK
    half = HEAD_DIM // 2
    full = lambda shape: pl.BlockSpec(shape, lambda i: (0,) * len(shape))
    inv_freq = ROPE_BASE ** (-jnp.arange(half, dtype=F32) / half)
    freq = jnp.tile(inv_freq, LANES // half).reshape(1, LANES)
    sgn = jnp.where((jnp.arange(LANES) % HEAD_DIM) < half, -1.0, 1.0).astype(F32).reshape(1, LANES)
    blk = (jnp.arange(W)[:, None] // HEAD_DIM == jnp.arange(W)[None, :] // HEAD_DIM).astype(BF16)
    return pl.pallas_call(
        _ret_kernel,
        grid=(t // Q,),
        in_specs=[pl.BlockSpec((bsz, Q, cols), lambda i: (0, i, 0)),
                  full((1, LANES)), full((1, LANES)), full((W, W))],
        out_specs=pl.BlockSpec((bsz, Q, W), lambda i: (0, i, 0)),
        out_shape=jax.ShapeDtypeStruct((bsz, t, W), BF16),
        scratch_shapes=[pltpu.VMEM((bsz * HEADS, HEAD_DIM, HEAD_DIM), F32), pltpu.VMEM((HEADS, Q, Q), F32)],
        compiler_params=_cparams("arbitrary"),
        name="retention",
    )(p_ret, freq, sgn, blk)


def _merge_kernel(x_ref, gl_ref, y0_ref, y1_ref, y2_ref, wb_ref, wo_ref, o_ref):
    d = x_ref.shape[1]
    merged = None
    for i, y_ref in enumerate((y0_ref, y1_ref, y2_ref)):
        gate = _sigmoid(gl_ref[:, i * d:(i + 1) * d].astype(F32))
        term = gate * jnp.dot(y_ref[...], wb_ref[i], preferred_element_type=F32)
        merged = term if merged is None else merged + term
    o_ref[...] = x_ref[...] + _dot(merged, wo_ref[...])


def _merge(x2, gate_logits, ys, w_branch, w_out, tm):
    n, d = x2.shape
    W = WIDTH
    rows = lambda c: pl.BlockSpec((tm, c), lambda i: (i, 0))
    return pl.pallas_call(
        _merge_kernel,
        grid=(n // tm,),
        in_specs=[rows(d), rows(3 * d), rows(W), rows(W), rows(W),
                  pl.BlockSpec((3, W, d), lambda i: (0, 0, 0)),
                  pl.BlockSpec((d, d), lambda i: (0, 0))],
        out_specs=rows(d),
        out_shape=jax.ShapeDtypeStruct((n, d), F32),
        compiler_params=_cparams("parallel"),
        name="merge_out",
    )(x2, gate_logits, *ys, w_branch.astype(BF16), w_out.astype(BF16))


def _ffn_kernel(x_ref, nw_ref, wg_ref, wu_ref, wd_ref, fw_ref, o_ref, h_s, acc_s, *, final_norm):
    j = pl.program_id(1)

    @pl.when(j == 0)
    def _():
        h_s[...] = _rms_rows(x_ref[...], nw_ref[...]).astype(BF16)
        acc_s[...] = jnp.zeros_like(acc_s)

    h = h_s[...]
    act = _silu(jnp.dot(h, wg_ref[...], preferred_element_type=F32)) * jnp.dot(
        h, wu_ref[...], preferred_element_type=F32)
    acc_s[...] += _dot(act, wd_ref[...])

    @pl.when(j == pl.num_programs(1) - 1)
    def _():
        y = x_ref[...] + acc_s[...]
        if final_norm:
            y = _rms_rows(y, fw_ref[...])
        o_ref[...] = y


def _ffn(x2, norm_w, wg, wu, wd, final_w, tm, tf):
    n, d = x2.shape
    f = wg.shape[1]
    final_norm = final_w is not None
    fw = (final_w if final_norm else jnp.ones((d,), F32)).reshape(1, d)
    return pl.pallas_call(
        functools.partial(_ffn_kernel, final_norm=final_norm),
        grid=(n // tm, f // tf),
        in_specs=[pl.BlockSpec((tm, d), lambda i, j: (i, 0)), pl.BlockSpec((1, d), lambda i, j: (0, 0)),
                  pl.BlockSpec((d, tf), lambda i, j: (0, j)), pl.BlockSpec((d, tf), lambda i, j: (0, j)),
                  pl.BlockSpec((tf, d), lambda i, j: (j, 0)), pl.BlockSpec((1, d), lambda i, j: (0, 0))],
        out_specs=pl.BlockSpec((tm, d), lambda i, j: (i, 0)),
        out_shape=jax.ShapeDtypeStruct((n, d), F32),
        scratch_shapes=[pltpu.VMEM((tm, d), BF16), pltpu.VMEM((tm, d), F32)],
        compiler_params=_cparams("parallel", "arbitrary"),
        name="ffn",
    )(x2, norm_w.reshape(1, d), wg, wu, wd, fw)


def _router_kernel(x_ref, nw_ref, rt_ref, h_ref, gate_ref, pos_ref, post_ref, cnt_ref, *, n_experts):
    tm = x_ref.shape[0]
    h = _rms_rows(x_ref[...], nw_ref[...])
    h_ref[...] = h.astype(BF16)
    r_hi, r_lo = rt_ref[0], rt_ref[1]
    h_hi, h_mid, h_lo = _split3(h)
    d = lambda a, b: jnp.dot(a, b, preferred_element_type=F32)
    logits = d(h_hi, r_hi) + d(h_mid, r_hi) + d(h_hi, r_lo) + d(h_lo, r_hi) + d(h_mid, r_lo)
    lane = lax.broadcasted_iota(jnp.int32, logits.shape, 1)
    neg = jnp.float32(-jnp.inf)
    logits = jnp.where(lane < n_experts, logits, neg)
    m1 = jnp.max(logits, axis=-1, keepdims=True)
    i1 = jnp.min(jnp.where(logits == m1, lane, LANES), axis=-1, keepdims=True)
    rest = jnp.where(lane == i1, neg, logits)
    m2 = jnp.max(rest, axis=-1, keepdims=True)
    i2 = jnp.min(jnp.where(rest == m2, lane, LANES), axis=-1, keepdims=True)
    e2 = jnp.exp(m2 - m1)
    w1 = 1.0 / (1.0 + e2)
    gate_ref[...] = jnp.where(lane == i1, w1, 0.0) + jnp.where(lane == i2, e2 * w1, 0.0)
    member = (lane == i1) | (lane == i2)
    m = jnp.where(member, 1.0, 0.0)
    row = lax.broadcasted_iota(jnp.int32, (tm, 1), 0)
    c = m
    step = 1
    while step < tm:
        c = c + jnp.where(row >= step, pltpu.roll(c, step, axis=0), 0.0)
        step *= 2
    posm = jnp.where(member, c - m, -1.0)
    pos_ref[...] = posm
    post_ref[0] = posm.T[0:SUBLANES, :]
    cnt_ref[0] = c[tm - 1:tm, :].astype(jnp.int32)


def _moe_gather_kernel(gi, ge, gr, gs, gf, h_ref, post_ref, xs_init_ref, xs_ref, *, br):
    w = pl.program_id(0)
    flags = gf[w]

    @pl.when((flags & 1) == 1)
    def _():
        rank = post_ref[0, pl.ds(ge[w], 1), :]
        slot = jnp.where(rank >= 0.0, rank + gs[w].astype(F32), -1.0)
        rows_i = lax.broadcasted_iota(jnp.int32, (br, 1), 0).astype(F32)
        sel = jnp.where(slot == rows_i, 1.0, 0.0).astype(BF16)
        rows = jnp.dot(sel, h_ref[...], preferred_element_type=F32).astype(BF16)

        @pl.when((flags & 2) == 2)
        def _():
            xs_ref[...] = rows

        @pl.when((flags & 2) == 0)
        def _():
            xs_ref[...] = (xs_ref[...].astype(F32) + rows.astype(F32)).astype(BF16)


def _moe_expert_kernel(eid, act, x_ref, wg_ref, wu_ref, wd_ref, o_ref, acc_s):
    r = pl.program_id(0)
    j = pl.program_id(1)
    last = j == pl.num_programs(1) - 1

    @pl.when(act[r] == 1)
    def _():
        x = x_ref[...]
        a = _silu(jnp.dot(x, wg_ref[0], preferred_element_type=F32)) * jnp.dot(
            x, wu_ref[0], preferred_element_type=F32)
        y = _dot(a, wd_ref[0])

        @pl.when(j == 0)
        def _():
            acc_s[...] = y

        @pl.when(j > 0)
        def _():
            acc_s[...] += y

        @pl.when(last)
        def _():
            o_ref[...] = acc_s[...].astype(o_ref.dtype)

    @pl.when((act[r] == 0) & last)
    def _():
        o_ref[...] = jnp.zeros_like(o_ref)


def _moe_combine_kernel(ci, ce, cr, cs, cf, x_ref, gate_ref, pos_ref, ys_ref, fw_ref, o_ref, *, br, final_norm):
    w = pl.program_id(0)
    flags = cf[w]
    tm = x_ref.shape[0]

    @pl.when((flags & 1) == 1)
    def _():
        @pl.when((flags & 2) == 2)
        def _():
            o_ref[...] = x_ref[...]

        lane = lax.broadcasted_iota(jnp.int32, (tm, LANES), 1)
        pick = lambda ref: jnp.sum(jnp.where(lane == ce[w], ref[...], 0.0), axis=-1, keepdims=True)
        rank = pick(pos_ref)
        slot = jnp.where(rank >= 0.0, rank + cs[w].astype(F32), -1.0)
        cols_i = lax.broadcasted_iota(jnp.int32, (1, br), 1).astype(F32)
        sel_t = jnp.where(slot == cols_i, 1.0, 0.0).astype(BF16)
        o_ref[...] += pick(gate_ref) * jnp.dot(sel_t, ys_ref[...], preferred_element_type=F32)

        if final_norm:
            @pl.when((flags & 4) == 4)
            def _():
                o_ref[...] = _rms_rows(o_ref[...], fw_ref[...])


def _moe_plan(cnt, tm, br, rm):
    nt, n_e = cnt.shape
    s_max = (-(-nt * tm * 2 // rm) + n_e) * rm
    tot = cnt.sum(0)
    ptot = (tot + rm - 1) // rm * rm
    seg_end = jnp.cumsum(ptot)
    off = (seg_end - ptot)[None, :] + jnp.cumsum(cnt, 0) - cnt
    first_blk = off // br
    nb = jnp.where(cnt > 0, (off + cnt - 1) // br - first_blk + 1, 0)
    w_max = s_max // br + nt * n_e
    w = jnp.arange(w_max, dtype=jnp.int32)

    def work_list(order):
        nb_o = nb.reshape(-1)[order]
        incl = jnp.cumsum(nb_o)
        total = incl[-1]
        w_eff = jnp.minimum(w, total - 1)
        p = jnp.minimum(jnp.sum(incl[None, :] <= w_eff[:, None], axis=1), nt * n_e - 1)
        pair = order[p]
        i_w, e_w = pair // n_e, pair % n_e
        r_w = first_blk.reshape(-1)[pair] + (w_eff - (incl - nb_o)[p])
        shift = off.reshape(-1)[pair] - r_w * br
        return (i_w, e_w, r_w, shift, w < total)

    ids = jnp.arange(nt * n_e, dtype=jnp.int32)
    gi, ge, gr, gs, gv = work_list(ids.reshape(nt, n_e).T.reshape(-1))
    g_first = gv & jnp.concatenate([jnp.ones((1,), bool), gr[1:] != gr[:-1]])
    g_flags = gv.astype(jnp.int32) + 2 * g_first.astype(jnp.int32)
    ci, ce, cr, cs, cv = work_list(ids)
    c_first = cv & jnp.concatenate([jnp.ones((1,), bool), ci[1:] != ci[:-1]])
    nxt_valid = jnp.concatenate([cv[1:], jnp.zeros((1,), bool)])
    c_last = cv & (jnp.concatenate([ci[1:] != ci[:-1], jnp.ones((1,), bool)]) | ~nxt_valid)
    c_flags = cv.astype(jnp.int32) + 2 * c_first.astype(jnp.int32) + 4 * c_last.astype(jnp.int32)
    r0 = jnp.arange(s_max // rm, dtype=jnp.int32) * rm
    eid = jnp.minimum(jnp.sum(seg_end[None, :] <= r0[:, None], axis=1), n_e - 1)
    act = (r0 < seg_end[-1]).astype(jnp.int32)
    i32 = lambda *xs: tuple(x.astype(jnp.int32) for x in xs)
    return s_max, i32(gi, ge, gr, gs, g_flags), i32(ci, ce, cr, cs, c_flags), i32(eid, act)


def _moe(x2, norm_w, wg, wu, wd, router, final_w, tm, tf, br, rm):
    n, d = x2.shape
    n_experts, _, f = wg.shape
    nt = n // tm
    final_norm = final_w is not None
    fw = (final_w if final_norm else jnp.ones((d,), F32)).reshape(1, d)
    rt = jnp.pad(router.astype(F32), ((0, 0), (0, LANES - n_experts)))
    r_hi = rt.astype(BF16)
    r_lo = (rt - r_hi.astype(F32)).astype(BF16)
    rows = lambda c: pl.BlockSpec((tm, c), lambda i: (i, 0))
    h, gates, pos, pos_t, cnt = pl.pallas_call(
        functools.partial(_router_kernel, n_experts=n_experts),
        grid=(nt,),
        in_specs=[rows(d), pl.BlockSpec((1, d), lambda i: (0, 0)),
                  pl.BlockSpec((2, d, LANES), lambda i: (0, 0, 0))],
        out_specs=[rows(d), rows(LANES), rows(LANES),
                   pl.BlockSpec((1, SUBLANES, tm), lambda i: (i, 0, 0)),
                   pl.BlockSpec((1, 1, LANES), lambda i: (i, 0, 0))],
        out_shape=[jax.ShapeDtypeStruct((n, d), BF16), jax.ShapeDtypeStruct((n, LANES), F32),
                   jax.ShapeDtypeStruct((n, LANES), F32), jax.ShapeDtypeStruct((nt, SUBLANES, tm), F32),
                   jax.ShapeDtypeStruct((nt, 1, LANES), jnp.int32)],
        compiler_params=_cparams("parallel"),
        name="moe_router",
    )(x2, norm_w.reshape(1, d), jnp.stack([r_hi, r_lo]))
    s_max, g_meta, c_meta, e_meta = _moe_plan(cnt[:, 0, :n_experts], tm, br, rm)
    n_work = g_meta[0].shape[0]

    xs = pl.pallas_call(
        functools.partial(_moe_gather_kernel, br=br),
        grid_spec=pltpu.PrefetchScalarGridSpec(
            num_scalar_prefetch=5, grid=(n_work,),
            in_specs=[pl.BlockSpec((tm, d), lambda w, gi, ge, gr, gs, gf: (gi[w], 0)),
                      pl.BlockSpec((1, SUBLANES, tm), lambda w, gi, ge, gr, gs, gf: (gi[w], 0, 0)),
                      pl.BlockSpec(memory_space=pl.ANY)],
            out_specs=pl.BlockSpec((br, d), lambda w, gi, ge, gr, gs, gf: (gr[w], 0))),
        out_shape=jax.ShapeDtypeStruct((s_max, d), BF16),
        input_output_aliases={7: 0},
        compiler_params=_cparams("arbitrary"),
        name="moe_gather",
    )(*g_meta, h, pos_t, jnp.zeros((s_max, d), BF16))

    nj = f // tf
    jj = lambda j, r, act: j * act[r] + (nj - 1) * (1 - act[r])
    ys = pl.pallas_call(
        _moe_expert_kernel,
        grid_spec=pltpu.PrefetchScalarGridSpec(
            num_scalar_prefetch=2, grid=(s_max // rm, nj),
            in_specs=[pl.BlockSpec((rm, d), lambda r, j, eid, act: (r, 0)),
                      pl.BlockSpec((1, d, tf), lambda r, j, eid, act: (eid[r], 0, jj(j, r, act))),
                      pl.BlockSpec((1, d, tf), lambda r, j, eid, act: (eid[r], 0, jj(j, r, act))),
                      pl.BlockSpec((1, tf, d), lambda r, j, eid, act: (eid[r], jj(j, r, act), 0))],
            out_specs=pl.BlockSpec((rm, d), lambda r, j, eid, act: (r, 0)),
            scratch_shapes=[pltpu.VMEM((rm, d), F32)]),
        out_shape=jax.ShapeDtypeStruct((s_max, d), BF16),
        compiler_params=_cparams("parallel", "arbitrary"),
        name="moe_experts",
    )(*e_meta, xs, wg, wu, wd)

    tile = lambda c: pl.BlockSpec((tm, c), lambda w, ci, ce, cr, cs, cf: (ci[w], 0))
    return pl.pallas_call(
        functools.partial(_moe_combine_kernel, br=br, final_norm=final_norm),
        grid_spec=pltpu.PrefetchScalarGridSpec(
            num_scalar_prefetch=5, grid=(n_work,),
            in_specs=[tile(d), tile(LANES), tile(LANES),
                      pl.BlockSpec((br, d), lambda w, ci, ce, cr, cs, cf: (cr[w], 0)),
                      pl.BlockSpec((1, d), lambda w, ci, ce, cr, cs, cf: (0, 0))],
            out_specs=tile(d)),
        out_shape=jax.ShapeDtypeStruct((n, d), F32),
        compiler_params=_cparams("arbitrary"),
        name="moe_combine",
    )(*c_meta, x2, gates, pos, ys, fw)


def _tiles(n_rows, t):
    tm = 512 if n_rows % 512 == 0 else 256
    tt = 256 if t % 256 == 0 else RW_CHUNK
    tm_moe = 1024 if n_rows % 1024 == 0 else tm
    br, rm = 512, 512
    return dict(tm=tm, tt=tt, tm_moe=tm_moe, br=br, rm=rm)


def _ff_tile(f, max_tile=2048):
    for parts in (2, 4, 7, 11, 14, 22, 28):
        if f % parts == 0 and (f // parts) % LANES == 0 and f // parts <= max_tile:
            return f // parts
    return f


def _deinterleave_heads(w):
    d_in = w.shape[0]
    return w.reshape(d_in, HEADS, HEAD_DIM // 2, 2).transpose(0, 1, 3, 2).reshape(d_in, WIDTH)


def kernel(x, norm1_w, w_in, rw_mu, rw_w_up, rw_w0, rw_a_up, rw_a0, rw_g_up, rw_k_k, rw_k_a, rw_r_k, rw_gn_w, rw_gn_b, rw_v_down, rw_v_up, rw_v0, ssm_conv_w, ssm_conv_b, ssm_dt_bias, ssm_a_log, ssm_d, ssm_norm_w, w_branch, w_out, norm2_w, ffn_wg, ffn_wu, ffn_wd, moe_router, moe_wg, moe_wu, moe_wd, final_norm_w):
    bsz, t, d = x.shape
    depth = w_in.shape[0]
    n = bsz * t
    W = WIDTH
    tl = _tiles(n, t)
    rw_cols = 3 * W + rw_w_up.shape[1] + rw_a_up.shape[1] + rw_g_up.shape[1]
    cdim = ssm_conv_w.shape[1]
    ssm_cols = W + cdim + ssm_a_log.shape[1]
    c1 = rw_cols
    c2 = c1 + ssm_cols
    c3 = c2 + 4 * W
    ssm_pad = (-ssm_cols) % LANES

    x2 = x.reshape(n, d)
    v_first = None
    for layer in range(depth):
        wl = w_in[layer]
        w_rw = wl[:, :c1].astype(BF16)
        w_ssm = jnp.pad(wl[:, c1:c2], ((0, 0), (0, ssm_pad))).astype(BF16)
        w_ret = jnp.concatenate([_deinterleave_heads(wl[:, c2:c2 + W]),
                                 _deinterleave_heads(wl[:, c2 + W:c2 + 2 * W]),
                                 wl[:, c2 + 2 * W:c3]], axis=1).astype(BF16)
        w_gate = wl[:, c3:].astype(BF16)
        p_rw = _norm_proj(x2, norm1_w[layer], w_rw, tl['tm'], BF16).reshape(bsz, t, -1)
        p_ssm = _norm_proj(x2, norm1_w[layer], w_ssm, tl['tm']).reshape(bsz, t, -1)
        p_ret = _norm_proj(x2, norm1_w[layer], w_ret, tl['tm'], BF16).reshape(bsz, t, -1)
        gate_logits = _norm_proj(x2, norm1_w[layer], w_gate, tl['tm'], BF16)

        rw_prm = dict(mu=rw_mu[layer], w_up=rw_w_up[layer], w0=rw_w0[layer], a_up=rw_a_up[layer],
                      a0=rw_a0[layer], g_up=rw_g_up[layer], k_k=rw_k_k[layer], k_a=rw_k_a[layer],
                      r_k=rw_r_k[layer].reshape(-1), gn_w=rw_gn_w[layer], gn_b=rw_gn_b[layer])
        vres = None if layer == 0 else (rw_v_down[layer - 1], rw_v_up[layer - 1], rw_v0[layer - 1])
        y_rw, v_first = _rwkv_mix(p_rw, rw_prm, v_first, vres, tl['tt'])
        ssm_prm = dict(conv_w=ssm_conv_w[layer], conv_b=ssm_conv_b[layer], dt_bias=ssm_dt_bias[layer],
                       a_log=ssm_a_log[layer], d=ssm_d[layer], norm_w=ssm_norm_w[layer])
        y_ssm = _ssm_mix(p_ssm, ssm_prm)
        y_ret = _ret_mix(p_ret)
        ys = [y.reshape(n, W) for y in (y_rw, y_ssm, y_ret)]
        x2 = _merge(x2, gate_logits, ys, w_branch[layer], w_out[layer], tl['tm'])

        j = layer // 2
        final_w = final_norm_w if layer == depth - 1 else None
        if layer % 2 == 0:
            x2 = _ffn(x2, norm2_w[layer], ffn_wg[j].astype(BF16), ffn_wu[j].astype(BF16),
                      ffn_wd[j].astype(BF16), final_w, tl['tm_moe'], _ff_tile(ffn_wg.shape[2]))
        else:
            x2 = _moe(x2, norm2_w[layer], moe_wg[j].astype(BF16), moe_wu[j].astype(BF16),
                      moe_wd[j].astype(BF16), moe_router[j], final_w, tl['tm_moe'],
                      _ff_tile(moe_wg.shape[3]), tl['br'], tl['rm'])
    return x2.reshape(bsz, t, d)
```
